```python
import math
import jax, jax.numpy as jnp
from jax import lax
import numpy as np

D_MODEL = 1024
BATCH = 32
SEQ = 256
DEPTH = 4
DEC_BATCH = 2
DEC_SEQ = 2048
PAST_LEN = 256

GRID_W = 64
N_MIXERS = 2
N_MLA = (DEPTH + 1) // 2
N_SSM = DEPTH // 2
N_MOD = 6
EPS = 1e-6

MLA_HEADS = 8
QK_NOPE = 128
QK_ROPE = 64
V_HEAD = 128
Q_LORA = 384
KV_LORA = 256
ROPE_THETA = 10000.0
ROPE_PAIRS_PER_AXIS = QK_ROPE // 4
Q_BLOCK = 128
ATTN_SCALE = (QK_NOPE + QK_ROPE) ** -0.5

D_INNER = 2 * D_MODEL
SSM_HEAD_DIM = 64
SSM_HEADS = D_INNER // SSM_HEAD_DIM
SSM_GROUPS = 4
D_STATE = 128
D_CONV = 3
CHUNK = 128
CONV_DIM = D_INNER + 2 * SSM_GROUPS * D_STATE
IN_PROJ_DIM = D_INNER + CONV_DIM + 2 * SSM_HEADS

N_EXPERTS = 16
D_FF_EXPERT = 1024
CAPACITY_FACTOR = 2

kernel_name = 'hybrid_mla_ssd_ecmoe_diffusion_step'


def rms_norm(x, g):
    xf = x.astype(jnp.float32)
    y = xf * lax.rsqrt(jnp.mean(xf * xf, axis=-1, keepdims=True) + EPS)
    return (y * g.astype(jnp.float32)).astype(x.dtype)


def modulation(cond, w_mod, b_mod):
    m = jax.nn.silu(cond) @ w_mod + b_mod
    return jnp.split(m, N_MOD, axis=-1)


def axial_rope_tables(grid_rows, dtype):
    row = jnp.repeat(jnp.arange(grid_rows), GRID_W).astype(jnp.float32)
    col = jnp.tile(jnp.arange(GRID_W), grid_rows).astype(jnp.float32)
    inv = ROPE_THETA ** (-jnp.arange(ROPE_PAIRS_PER_AXIS, dtype=jnp.float32) / ROPE_PAIRS_PER_AXIS)
    ang = jnp.concatenate([row[:, None] * inv, col[:, None] * inv], axis=-1)
    return jnp.cos(ang).astype(dtype), jnp.sin(ang).astype(dtype)


def apply_rope(x, cos, sin):
    half = QK_ROPE // 2
    x1, x2 = x[..., :half], x[..., half:]
    return jnp.concatenate([x1 * cos - x2 * sin, x2 * cos + x1 * sin], axis=-1)


def mla_project(h, w_dq, q_norm_g, w_uq, w_dkv, kv_norm_g):
    b, l, _ = h.shape
    q = (rms_norm(h @ w_dq, q_norm_g) @ w_uq).reshape(b, l, MLA_HEADS, QK_NOPE + QK_ROPE)
    kv = h @ w_dkv
    ckv = rms_norm(kv[..., :KV_LORA], kv_norm_g)
    return q[..., :QK_NOPE], q[..., QK_NOPE:], ckv, kv[..., KV_LORA:]


def mla_expand(ckv, w_ukv):
    b, s, _ = ckv.shape
    kv = (ckv @ w_ukv).reshape(b, s, MLA_HEADS, QK_NOPE + V_HEAD)
    return kv[..., :QK_NOPE], kv[..., QK_NOPE:]


def mla_attention(q_nope, q_rope, k_nope, k_rope, v):
    b, l, h, _ = q_nope.shape
    nb = l // Q_BLOCK
    qn = jnp.moveaxis(q_nope.reshape(b, nb, Q_BLOCK, h, QK_NOPE), 1, 0)
    qr = jnp.moveaxis(q_rope.reshape(b, nb, Q_BLOCK, h, QK_ROPE), 1, 0)

    def block(args):
        qn_b, qr_b = args
        s = (jnp.einsum('bqhd,bkhd->bhqk', qn_b, k_nope).astype(jnp.float32)
             + jnp.einsum('bqhd,bkd->bhqk', qr_b, k_rope).astype(jnp.float32)) * ATTN_SCALE
        p = jax.nn.softmax(s, axis=-1).astype(v.dtype)
        return jnp.einsum('bhqk,bkhd->bqhd', p, v)

    o = lax.map(block, (qn, qr))
    return jnp.moveaxis(o, 0, 1).reshape(b, l, h * V_HEAD)


def mla_context(h, w_dq, q_norm_g, w_uq, w_dkv, kv_norm_g, w_ukv, w_o):
    q_nope, q_rope, ckv, k_rope = mla_project(h, w_dq, q_norm_g, w_uq, w_dkv, kv_norm_g)
    k_nope, v = mla_expand(ckv, w_ukv)
    o = mla_attention(q_nope, q_rope, k_nope, k_rope, v)
    return o @ w_o, ckv, k_rope


def mla_latent(h, ckv_ctx, krope_ctx, cos, sin, w_dq, q_norm_g, w_uq, w_dkv, kv_norm_g, w_ukv, w_o):
    q_nope, q_rope, ckv, k_rope = mla_project(h, w_dq, q_norm_g, w_uq, w_dkv, kv_norm_g)
    q_rope = apply_rope(q_rope, cos[:, None, :], sin[:, None, :])
    k_rope = apply_rope(k_rope, cos, sin)
    ckv_all = jnp.concatenate([ckv_ctx.astype(ckv.dtype), ckv], axis=1)
    krope_all = jnp.concatenate([krope_ctx.astype(k_rope.dtype), k_rope], axis=1)
    k_nope, v = mla_expand(ckv_all, w_ukv)
    o = mla_attention(q_nope, q_rope, k_nope, krope_all, v)
    return o @ w_o


def centred_depthwise_conv(x, conv_w, conv_b):
    pad = D_CONV // 2
    w = conv_w.T[:, None, :].astype(x.dtype)
    y = lax.conv_general_dilated(x, w, window_strides=(1,), padding=[(pad, pad)],
                                 dimension_numbers=('NWC', 'WIO', 'NWC'),
                                 feature_group_count=x.shape[-1])
    return y + conv_b.astype(x.dtype)


def ssd_chunked_scan(x, dt, a, b_in, c_in, h0):
    bsz, l, h, p = x.shape
    g, n = SSM_GROUPS, D_STATE
    r = h // g
    nc = l // CHUNK
    f32 = jnp.float32
    xc = x.astype(f32).reshape(bsz, nc, CHUNK, g, r, p)
    dtc = dt.astype(f32).reshape(bsz, nc, CHUNK, g, r)
    bc = b_in.astype(f32).reshape(bsz, nc, CHUNK, g, n)
    cc = c_in.astype(f32).reshape(bsz, nc, CHUNK, g, n)
    a_cs = jnp.cumsum(dtc * a.reshape(g, r), axis=2)
    lower = jnp.tril(jnp.ones((CHUNK, CHUNK), dtype=bool))
    seg = a_cs[:, :, :, None] - a_cs[:, :, None, :]
    decay = jnp.exp(jnp.where(lower[:, :, None, None], seg, -jnp.inf))
    cb = jnp.einsum('bcign,bcjgn->bcijg', cc, bc)
    w = cb[..., None] * decay * dtc[:, :, None]
    y_diag = jnp.einsum('bcijgr,bcjgrp->bcigrp', w, xc)
    to_end = jnp.exp(a_cs[:, :, -1:] - a_cs) * dtc
    states = jnp.einsum('bcqgn,bcqgr,bcqgrp->bcgrpn', bc, to_end, xc)
    chunk_decay = jnp.exp(a_cs[:, :, -1])

    def step(state, inp):
        dec, st = inp
        return dec[..., None, None] * state + st, state

    h_last, h_in = lax.scan(step, h0.astype(f32).reshape(bsz, g, r, p, n),
                            (jnp.moveaxis(chunk_decay, 1, 0), jnp.moveaxis(states, 1, 0)))
    h_in = jnp.moveaxis(h_in, 0, 1)
    y_off = jnp.einsum('bcqgn,bcgrpn->bcqgrp', cc, h_in) * jnp.exp(a_cs)[..., None]
    y = (y_diag + y_off).reshape(bsz, l, h, p)
    return y, h_last.reshape(bsz, h, p, n)


def bidir_mamba2(h, h0_fwd, h0_bwd, w_in, conv_w, conv_b, dt_bias, a_log, d_skip, norm_g, w_out):
    bsz, l, _ = h.shape
    gn = SSM_GROUPS * D_STATE
    zxbcdt = h @ w_in
    z = zxbcdt[..., :D_INNER]
    xbc = jax.nn.silu(centred_depthwise_conv(zxbcdt[..., D_INNER:D_INNER + CONV_DIM], conv_w, conv_b))
    dt_raw = zxbcdt[..., D_INNER + CONV_DIM:]
    x = xbc[..., :D_INNER].reshape(bsz, l, SSM_HEADS, SSM_HEAD_DIM)
    b_in = xbc[..., D_INNER:D_INNER + gn].reshape(bsz, l, SSM_GROUPS, D_STATE)
    c_in = xbc[..., D_INNER + gn:].reshape(bsz, l, SSM_GROUPS, D_STATE)
    dt = jax.nn.softplus(dt_raw.astype(jnp.float32) + dt_bias.reshape(-1).astype(jnp.float32))
    dt = dt.reshape(bsz, l, 2, SSM_HEADS)
    a = -jnp.exp(a_log.astype(jnp.float32))
    flip = lambda t: jnp.flip(t, axis=1)
    y_f, h_f = ssd_chunked_scan(x, dt[:, :, 0], a[0], b_in, c_in, h0_fwd)
    y_b, h_b = ssd_chunked_scan(flip(x), flip(dt[:, :, 1]), a[1], flip(b_in), flip(c_in), h0_bwd)
    y = y_f + flip(y_b) + d_skip.astype(jnp.float32)[:, None] * x.astype(jnp.float32)
    y = y.reshape(bsz, l, D_INNER) * jax.nn.silu(z.astype(jnp.float32))
    y = rms_norm(y, norm_g).astype(h.dtype)
    return y @ w_out, h_f, h_b


def expert_choice_moe(h, w_router, w_gate, w_up, w_down):
    bsz, l, d = h.shape
    t = h.reshape(bsz * l, d)
    cap = CAPACITY_FACTOR * (bsz * l) // N_EXPERTS
    aff = jax.nn.softmax((t @ w_router).astype(jnp.float32), axis=-1)
    gates, idx = lax.top_k(aff.T, cap)
    xe = t[idx]
    hid = jax.nn.silu(jnp.einsum('ecd,edf->ecf', xe, w_gate)) * jnp.einsum('ecd,edf->ecf', xe, w_up)
    ye = jnp.einsum('ecf,efd->ecd', hid, w_down) * gates[..., None].astype(h.dtype)
    out = jnp.zeros_like(t).at[idx.reshape(-1)].add(ye.reshape(-1, d))
    return out.reshape(bsz, l, d)


def setup_inputs(seed: int = 0) -> dict:
    key = jax.random.key(seed)
    keys = iter(jax.random.split(key, 40))
    f32 = jnp.float32

    def nrm(shape, scale=1.0):
        return jax.random.normal(next(keys), shape, f32) * scale

    def gain(shape):
        return 1.0 + nrm(shape, 0.02)

    D = D_MODEL
    H = SSM_HEADS
    dt0 = jnp.exp(jax.random.uniform(next(keys), (N_SSM, 2, H), f32, math.log(1e-3), math.log(1e-1)))
    dt_bias = dt0 + jnp.log(-jnp.expm1(-dt0))
    a_log = jnp.log(jax.random.uniform(next(keys), (N_SSM, 2, H), f32, 1.0, 16.0))
    return {
        'x_prompt': nrm((BATCH, SEQ, D)),
        'x_sample': nrm((DEC_BATCH, DEC_SEQ, D)),
        'cache_ckv': nrm((DEC_BATCH, N_MLA, PAST_LEN, KV_LORA)),
        'cache_krope': nrm((DEC_BATCH, N_MLA, PAST_LEN, QK_ROPE)),
        'state_ssm': nrm((DEC_BATCH, N_SSM, 2, H, SSM_HEAD_DIM, D_STATE), 0.5),
        'c': nrm((DEC_BATCH, D)),
        'c_ctx': nrm((D,)),
        'w_mod': nrm((DEPTH, D, N_MOD * D), D ** -0.5),
        'b_mod': nrm((DEPTH, N_MOD * D), 0.01),
        'norm1_g': gain((DEPTH, D)),
        'norm2_g': gain((DEPTH, D)),
        'final_norm_g': gain((D,)),
        'mla_w_dq': nrm((N_MLA, D, Q_LORA), D ** -0.5),
        'mla_q_norm_g': gain((N_MLA, Q_LORA)),
        'mla_w_uq': nrm((N_MLA, Q_LORA, MLA_HEADS * (QK_NOPE + QK_ROPE)), Q_LORA ** -0.5),
        'mla_w_dkv': nrm((N_MLA, D, KV_LORA + QK_ROPE), D ** -0.5),
        'mla_kv_norm_g': gain((N_MLA, KV_LORA)),
        'mla_w_ukv': nrm((N_MLA, KV_LORA, MLA_HEADS * (QK_NOPE + V_HEAD)), KV_LORA ** -0.5),
        'mla_w_o': nrm((N_MLA, MLA_HEADS * V_HEAD, D), (MLA_HEADS * V_HEAD) ** -0.5),
        'ssm_w_in': nrm((N_SSM, D, IN_PROJ_DIM), D ** -0.5),
        'ssm_conv_w': nrm((N_SSM, CONV_DIM, D_CONV), D_CONV ** -0.5),
        'ssm_conv_b': nrm((N_SSM, CONV_DIM), 0.01),
        'ssm_dt_bias': dt_bias,
        'ssm_a_log': a_log,
        'ssm_d_skip': 1.0 + nrm((N_SSM, H), 0.1),
        'ssm_norm_g': gain((N_SSM, D_INNER)),
        'ssm_w_out': nrm((N_SSM, D_INNER, D), D_INNER ** -0.5),
        'moe_w_router': nrm((DEPTH, D, N_EXPERTS), D ** -0.5),
        'moe_w_gate': nrm((DEPTH, N_EXPERTS, D, D_FF_EXPERT), D ** -0.5),
        'moe_w_up': nrm((DEPTH, N_EXPERTS, D, D_FF_EXPERT), D ** -0.5),
        'moe_w_down': nrm((DEPTH, N_EXPERTS, D_FF_EXPERT, D), D_FF_EXPERT ** -0.5),
    }


def reference(x_prompt, x_sample, cache_ckv, cache_krope, state_ssm, c, c_ctx,
              w_mod, b_mod, norm1_g, norm2_g, final_norm_g,
              mla_w_dq, mla_q_norm_g, mla_w_uq, mla_w_dkv, mla_kv_norm_g, mla_w_ukv, mla_w_o,
              ssm_w_in, ssm_conv_w, ssm_conv_b, ssm_dt_bias, ssm_a_log, ssm_d_skip, ssm_norm_g, ssm_w_out,
              moe_w_router, moe_w_gate, moe_w_up, moe_w_down):
    grid_rows = x_sample.shape[1] // GRID_W
    cos, sin = axial_rope_tables(grid_rows, x_sample.dtype)
    xp, xs = x_prompt, x_sample
    new_ckv, new_krope, new_ssm = [], [], []
    for l in range(DEPTH):
        sh1p, sc1p, g1p, sh2p, sc2p, g2p = modulation(c_ctx, w_mod[l], b_mod[l])
        sh1s, sc1s, g1s, sh2s, sc2s, g2s = modulation(c[:, None, :], w_mod[l], b_mod[l])
        hp = rms_norm(xp, norm1_g[l]) * (1.0 + sc1p) + sh1p
        hs = rms_norm(xs, norm1_g[l]) * (1.0 + sc1s) + sh1s
        j = l // N_MIXERS
        if l % N_MIXERS == 0:
            mla = (mla_w_dq[j], mla_q_norm_g[j], mla_w_uq[j], mla_w_dkv[j], mla_kv_norm_g[j], mla_w_ukv[j], mla_w_o[j])
            op, ckv, krope = mla_context(hp, *mla)
            os_ = mla_latent(hs, cache_ckv[:, j], cache_krope[:, j], cos, sin, *mla)
            new_ckv.append(ckv)
            new_krope.append(krope)
        else:
            ssm = (ssm_w_in[j], ssm_conv_w[j], ssm_conv_b[j], ssm_dt_bias[j], ssm_a_log[j], ssm_d_skip[j], ssm_norm_g[j], ssm_w_out[j])
            zero = jnp.zeros((xp.shape[0], SSM_HEADS, SSM_HEAD_DIM, D_STATE), jnp.float32)
            op, h_f, h_b = bidir_mamba2(hp, zero, zero, *ssm)
            os_, _, _ = bidir_mamba2(hs, state_ssm[:, j, 0], state_ssm[:, j, 1], *ssm)
            new_ssm.append(jnp.stack([h_f, h_b], axis=1))
        xp = xp + g1p * op
        xs = xs + g1s * os_
        moe = (moe_w_router[l], moe_w_gate[l], moe_w_up[l], moe_w_down[l])
        xp = xp + g2p * expert_choice_moe(rms_norm(xp, norm2_g[l]) * (1.0 + sc2p) + sh2p, *moe)
        xs = xs + g2s * expert_choice_moe(rms_norm(xs, norm2_g[l]) * (1.0 + sc2s) + sh2s, *moe)
    y_prompt = rms_norm(xp, final_norm_g)
    y_sample = rms_norm(xs, final_norm_g)
    new_ckv_arr = jnp.stack(new_ckv, axis=1)
    new_krope_arr = jnp.stack(new_krope, axis=1)
    new_ssm_arr = jnp.stack(new_ssm, axis=1)
    return (y_prompt, y_sample, new_ckv_arr, new_krope_arr, new_ssm_arr)
```

```python
import functools

import jax
import jax.numpy as jnp
from jax import lax
from jax.experimental import pallas as pl
from jax.experimental.pallas import tpu as pltpu

f32 = jnp.float32
bf16 = jnp.bfloat16
i32 = jnp.int32

D = 1024
DEPTH = 4
N_MOD = 6
EPS = 1e-6
GRID_W = 64

MLA_HEADS = 8
QK_NOPE = 128
QK_ROPE = 64
V_HEAD = 128
Q_LORA = 384
KV_LORA = 256
ROPE_THETA = 10000.0
ATTN_SCALE = (QK_NOPE + QK_ROPE) ** -0.5
HEAD_PAD = 128

D_INNER = 2 * D
SSM_P = 64
SSM_H = D_INNER // SSM_P
SSM_G = 4
SSM_N = 128
CHUNK = 128
CONV_DIM = D_INNER + 2 * SSM_G * SSM_N
HEADS_PER_GROUP = SSM_H // SSM_G
GROUP_W = HEADS_PER_GROUP * SSM_P

N_EXPERTS = 16
D_FF = 1024

VMEM_LIMIT = 56 * 1024 * 1024

SHIFT1, SCALE1, GATE1, SHIFT2, SCALE2, GATE2 = range(6)


def _cparams(*sem):
    return pltpu.CompilerParams(dimension_semantics=sem, vmem_limit_bytes=VMEM_LIMIT)


def _dot(a, b):
    return jnp.dot(a, b, preferred_element_type=f32)


def _dot_nt(a, b):
    return lax.dot_general(a, b, (((1,), (1,)), ((), ())), preferred_element_type=f32)


def _dot_tn(a, b):
    return lax.dot_general(a, b, (((0,), (0,)), ((), ())), preferred_element_type=f32)


def _dot_f32(a, b):
    return jnp.dot(a, b, preferred_element_type=f32, precision=lax.Precision.HIGHEST)


def _rms(x, g):
    ms = jnp.mean(x * x, axis=-1, keepdims=True)
    return x * lax.rsqrt(ms + EPS) * g


def _norm_mod(x, g, mod, k_shift, k_scale):
    return _rms(x, g) * (1.0 + mod[k_scale:k_scale + 1, :]) + mod[k_shift:k_shift + 1, :]


def _silu(x):
    return x * jax.nn.sigmoid(x)


def _mod_kernel(c_ref, w_ref, b_ref, o_ref):
    s = _silu(c_ref[...]).astype(bf16)
    o_ref[0] = _dot(s, w_ref[0].astype(bf16)) + b_ref[0]


def modulation_all(cond8, w_mod, b_mod):
    nb = 1536
    n = N_MOD * D
    return pl.pallas_call(
        _mod_kernel,
        grid=(DEPTH, n // nb),
        in_specs=[
            pl.BlockSpec((8, D), lambda l, j: (0, 0)),
            pl.BlockSpec((1, D, nb), lambda l, j: (l, 0, j)),
            pl.BlockSpec((1, 1, nb), lambda l, j: (l, 0, j)),
        ],
        out_specs=pl.BlockSpec((1, 8, nb), lambda l, j: (l, 0, j)),
        out_shape=jax.ShapeDtypeStruct((DEPTH, 8, n), f32),
        compiler_params=_cparams("arbitrary", "arbitrary"),
        name="modulation",
    )(cond8, w_mod, b_mod.reshape(DEPTH, 1, n))


def _rope_rot(p, c, s1, s2):
    return p * c + pltpu.roll(p, 96, axis=1) * s1 + pltpu.roll(p, 32, axis=1) * s2


def _mla_proj_kernel(*refs, rope):
    if rope:
        (x_ref, g_ref, mod_ref, wdq_ref, qg_ref, wuq_ref, wdkv_ref, kvg_ref, wukv_ref, rc_ref, rs1_ref, rs2_ref,
         qn_ref, qr_ref, kn_ref, v_ref, kr_ref, ckv_ref, krope_ref) = refs
    else:
        (x_ref, g_ref, mod_ref, wdq_ref, qg_ref, wuq_ref, wdkv_ref, kvg_ref, wukv_ref,
         qn_ref, qr_ref, kn_ref, v_ref, kr_ref, ckv_ref, krope_ref) = refs
    h = _norm_mod(x_ref[...], g_ref[...], mod_ref[0], SHIFT1, SCALE1).astype(bf16)
    nq = MLA_HEADS * QK_NOPE
    q_lat = _rms(_dot(h, wdq_ref[...]), qg_ref[...]).astype(bf16)
    q = _dot(q_lat, wuq_ref[...])
    qn_ref[...] = q[:, :nq].astype(bf16)
    if rope:
        c, s1, s2 = rc_ref[...], rs1_ref[...], rs2_ref[...]
    for hh in range(MLA_HEADS):
        piece = q[:, nq + hh * HEAD_PAD: nq + (hh + 1) * HEAD_PAD]
        if rope:
            piece = _rope_rot(piece, c, s1, s2)
        qr_ref[:, hh * HEAD_PAD:(hh + 1) * HEAD_PAD] = piece.astype(bf16)
    kv = _dot(h, wdkv_ref[...])
    ckv = _rms(kv[:, :KV_LORA], kvg_ref[...])
    ckv_ref[...] = ckv
    kr = kv[:, KV_LORA:]
    krope_ref[...] = kr
    if rope:
        kr = _rope_rot(kr, c, s1, s2)
    kr_ref[...] = kr.astype(bf16)
    kvx = _dot(ckv.astype(bf16), wukv_ref[...])
    nk = MLA_HEADS * QK_NOPE
    kn_ref[...] = kvx[:, :nk].astype(bf16)
    v_ref[...] = kvx[:, nk:].astype(bf16)


def mla_project(x, g, mod, w, rope_tabs, rows_per_mod, tm=256):
    t = x.shape[0]
    wdq, qg, wuq, wdkv, kvg, wukv = w
    full = lambda a: pl.BlockSpec(a.shape, lambda i: (0,) * a.ndim)
    row = lambda n: pl.BlockSpec((tm, n), lambda i: (i, 0))
    in_specs = [row(D), full(g), pl.BlockSpec((1, N_MOD, D), lambda i: ((i * tm) // rows_per_mod, 0, 0)),
                full(wdq), full(qg), full(wuq), full(wdkv), full(kvg), full(wukv)]
    args = [x, g, mod, wdq, qg, wuq, wdkv, kvg, wukv]
    rope = rope_tabs is not None
    if rope:
        nrb = rope_tabs[0].shape[0] // tm
        in_specs += [pl.BlockSpec((tm, HEAD_PAD), lambda i: (i % nrb, 0))] * 3
        args += list(rope_tabs)
    nh = MLA_HEADS * HEAD_PAD
    out_shape = [jax.ShapeDtypeStruct((t, nh), bf16)] * 4 + [
        jax.ShapeDtypeStruct((t, HEAD_PAD), bf16), jax.ShapeDtypeStruct((t, KV_LORA), f32),
        jax.ShapeDtypeStruct((t, HEAD_PAD), f32)]
    out_specs = [row(nh)] * 4 + [row(HEAD_PAD), row(KV_LORA), row(HEAD_PAD)]
    return pl.pallas_call(
        functools.partial(_mla_proj_kernel, rope=rope),
        grid=(t // tm,), in_specs=in_specs, out_specs=out_specs, out_shape=out_shape,
        compiler_params=_cparams("arbitrary"), name="mla_project",
    )(*args)


def _mm_kernel(a_ref, w_ref, o_ref):
    o_ref[...] = _dot(a_ref[...], w_ref[...]).astype(o_ref.dtype)


def matmul_bf16(a, w, tm):
    m, k = a.shape
    n = w.shape[1]
    return pl.pallas_call(
        _mm_kernel, grid=(m // tm,),
        in_specs=[pl.BlockSpec((tm, k), lambda i: (i, 0)), pl.BlockSpec((k, n), lambda i: (0, 0))],
        out_specs=pl.BlockSpec((tm, n), lambda i: (i, 0)),
        out_shape=jax.ShapeDtypeStruct((m, n), bf16),
        compiler_params=_cparams("arbitrary"), name="matmul_bf16",
    )(a, w)


def _attn_kernel(qn_ref, qr_ref, kn_ref, kr_ref, v_ref, o_ref):
    kr = kr_ref[0]
    for h in range(MLA_HEADS):
        sl = slice(h * HEAD_PAD, (h + 1) * HEAD_PAD)
        s = (_dot_nt(qn_ref[0, :, sl], kn_ref[0, :, sl]) + _dot_nt(qr_ref[0, :, sl], kr)) * ATTN_SCALE
        e = jnp.exp(s - jnp.max(s, axis=-1, keepdims=True))
        l = jnp.sum(e, axis=-1, keepdims=True)
        o = _dot(e.astype(bf16), v_ref[0, :, sl]) / l
        o_ref[0, :, sl] = o.astype(bf16)


def mla_attention(qn, qr, kn, kr, v, tq=256):
    b, lq, nh = qn.shape
    s = kn.shape[1]
    qspec = pl.BlockSpec((1, tq, nh), lambda i, j: (i, j, 0))
    kspec = pl.BlockSpec((1, s, nh), lambda i, j: (i, 0, 0))
    return pl.pallas_call(
        _attn_kernel, grid=(b, lq // tq),
        in_specs=[qspec, qspec, kspec, pl.BlockSpec((1, s, HEAD_PAD), lambda i, j: (i, 0, 0)), kspec],
        out_specs=qspec, out_shape=jax.ShapeDtypeStruct((b, lq, nh), bf16),
        compiler_params=_cparams("arbitrary", "arbitrary"), name="mla_attention",
    )(qn, qr, kn, kr, v)


def _mm_res_kernel(a_ref, w_ref, x_ref, mod_ref, o_ref, *, kgate):
    y = _dot(a_ref[...], w_ref[...])
    o_ref[...] = x_ref[...] + mod_ref[0][kgate:kgate + 1, :] * y


def matmul_residual(a, w, x, mod, kgate, rows_per_mod, tm=512):
    t, k = a.shape
    return pl.pallas_call(
        functools.partial(_mm_res_kernel, kgate=kgate), grid=(t // tm,),
        in_specs=[pl.BlockSpec((tm, k), lambda i: (i, 0)), pl.BlockSpec((k, D), lambda i: (0, 0)),
                  pl.BlockSpec((tm, D), lambda i: (i, 0)),
                  pl.BlockSpec((1, N_MOD, D), lambda i: ((i * tm) // rows_per_mod, 0, 0))],
        out_specs=pl.BlockSpec((tm, D), lambda i: (i, 0)),
        out_shape=jax.ShapeDtypeStruct((t, D), f32),
        compiler_params=_cparams("arbitrary"), name="matmul_residual",
    )(a, w, x, mod)


def _inproj_kernel(*refs, kind, period):
    if kind == "conv":
        x_ref, g_ref, mod_ref, w_ref, cw_ref, cb_ref, o_ref, h_ref = refs
    elif kind == "dt":
        x_ref, g_ref, mod_ref, w_ref, cb_ref, o_ref, h_ref = refs
    else:
        x_ref, g_ref, mod_ref, w_ref, o_ref, h_ref = refs

    @pl.when(pl.program_id(1) == 0)
    def _():
        h_ref[...] = _norm_mod(x_ref[...], g_ref[...], mod_ref[0], SHIFT1, SCALE1).astype(bf16)

    y = _dot(h_ref[...], w_ref[...])
    if kind == "conv":
        tm = y.shape[0]
        pos = lax.broadcasted_iota(i32, (tm, 1), 0) & (period - 1)
        prev = jnp.where(pos == 0, 0.0, pltpu.roll(y, 1, axis=0))
        nxt = jnp.where(pos == period - 1, 0.0, pltpu.roll(y, tm - 1, axis=0))
        cw = cw_ref[...]
        y = _silu(cw[0:1, :] * prev + cw[1:2, :] * y + cw[2:3, :] * nxt + cb_ref[...])
    elif kind == "dt":
        y = y + cb_ref[...]
        y = jnp.maximum(y, 0.0) + jnp.log1p(jnp.exp(-jnp.abs(y)))
    o_ref[...] = y


def ssm_in_proj(x, g, mod, w, rows_per_mod, kind, period=None, conv_w=None, bias=None, tm=2048, nb=512):
    t = x.shape[0]
    n = w.shape[1]
    nb = min(nb, n)
    in_specs = [pl.BlockSpec((tm, D), lambda i, j: (i, 0)), pl.BlockSpec((1, D), lambda i, j: (0, 0)),
                pl.BlockSpec((1, N_MOD, D), lambda i, j: ((i * tm) // rows_per_mod, 0, 0)),
                pl.BlockSpec((D, nb), lambda i, j: (0, j))]
    args = [x, g, mod, w]
    if kind == "conv":
        in_specs += [pl.BlockSpec((3, nb), lambda i, j: (0, j)), pl.BlockSpec((1, nb), lambda i, j: (0, j))]
        args += [conv_w, bias]
    elif kind == "dt":
        in_specs += [pl.BlockSpec((1, nb), lambda i, j: (0, j))]
        args += [bias]
    return pl.pallas_call(
        functools.partial(_inproj_kernel, kind=kind, period=period), grid=(t // tm, n // nb),
        in_specs=in_specs, out_specs=pl.BlockSpec((tm, nb), lambda i, j: (i, j)),
        out_shape=jax.ShapeDtypeStruct((t, n), f32),
        scratch_shapes=[pltpu.VMEM((tm, D), bf16)],
        compiler_params=_cparams("arbitrary", "arbitrary"), name="ssm_in_proj_" + kind,
    )(*args)


def _ssd_kernel(*refs, nc, has_h0):
    if has_h0:
        (xbc_ref, z_ref, dt_ref, dtT_ref, bT_ref, alr_ref, alc_ref, dsk_ref, ng_ref, h0_ref,
         y_ref, hout_ref, state_ref, yf_ref, xs_ref, yc_ref, yo_ref, dec_ref, col_ref, row_ref) = refs
    else:
        (xbc_ref, z_ref, dt_ref, dtT_ref, bT_ref, alr_ref, alc_ref, dsk_ref, ng_ref,
         y_ref, hout_ref, state_ref, yf_ref, xs_ref, yc_ref, yo_ref, dec_ref, col_ref, row_ref) = refs
    q = CHUNK
    d = pl.program_id(1)
    c = pl.program_id(2)
    ce = jnp.where(d == 0, c, nc - 1 - c)

    @pl.when(c == 0)
    def _():
        if has_h0:
            state_ref[...] = h0_ref[0, 0]
        else:
            state_ref[...] = jnp.zeros_like(state_ref)

    ii = lax.broadcasted_iota(i32, (q, q), 0)
    jj = lax.broadcasted_iota(i32, (q, q), 1)
    ahead = (ii - jj) * jnp.where(d == 0, 1, -1)
    causal = ahead >= 0
    m_col = jnp.where(causal, 1.0, 0.0).astype(f32)
    m_row = jnp.where(ahead <= 0, 1.0, 0.0).astype(f32)

    a_row = -jnp.exp(alr_ref[0])
    a_col = -jnp.exp(alc_ref[0])
    dt = dt_ref[0, 0]
    dtT = dtT_ref[0, 0]
    da = dt * a_row
    cs_col = _dot_f32(m_col, da)
    cs_row = _dot_f32(dtT * a_col, m_row)
    tot = jnp.sum(da, axis=0, keepdims=True)
    col_ref[0] = cs_col
    col_ref[1] = jnp.exp(cs_col)
    col_ref[2] = jnp.exp(tot - cs_col) * dt
    row_ref[0] = cs_row
    row_ref[1] = dtT
    dec = jnp.exp(tot)

    for g in range(SSM_G):
        gs = slice(g * GROUP_W, (g + 1) * GROUP_W)
        b_g = xbc_ref[0, :, D_INNER + g * SSM_N: D_INNER + (g + 1) * SSM_N].astype(bf16)
        c_off = D_INNER + SSM_G * SSM_N
        c_g = xbc_ref[0, :, c_off + g * SSM_N: c_off + (g + 1) * SSM_N].astype(bf16)
        cb = _dot_nt(c_g, b_g)
        yo_ref[...] = _dot(c_g, state_ref[:, gs].astype(bf16))
        for hl in range(HEADS_PER_GROUP):
            h = g * HEADS_PER_GROUP + hl
            hs = slice(h * SSM_P, (h + 1) * SSM_P)
            seg = col_ref[0, :, h:h + 1] - row_ref[0, h:h + 1, :]
            w = cb * jnp.exp(jnp.where(causal, seg, -jnp.inf)) * row_ref[1, h:h + 1, :]
            xh = xbc_ref[0, :, hs]
            yd = _dot(w.astype(bf16), xh.astype(bf16))
            xs_ref[:, hs] = (xh * col_ref[2, :, h:h + 1]).astype(bf16)
            yc_ref[:, hs] = yd + yo_ref[:, hl * SSM_P:(hl + 1) * SSM_P] * col_ref[1, :, h:h + 1]
            dec_ref[0:1, hs] = jnp.broadcast_to(dec[:, h:h + 1], (1, SSM_P))
        bT_g = bT_ref[0, g * SSM_N:(g + 1) * SSM_N, :].astype(bf16)
        state_ref[:, gs] = state_ref[:, gs] * dec_ref[0:1, gs] + _dot(bT_g, xs_ref[:, gs])

    rows = pl.ds(pl.multiple_of(ce * q, q), q)

    @pl.when(d == 0)
    def _():
        yf_ref[rows, :] = yc_ref[...]

    @pl.when(d == 1)
    def _():
        y = yf_ref[rows, :] + yc_ref[...] + dsk_ref[...] * xbc_ref[0, :, :D_INNER]
        y = y * _silu(z_ref[0])
        y_ref[0] = _rms(y, ng_ref[...]).astype(bf16)

    @pl.when(c == nc - 1)
    def _():
        hout_ref[0, 0] = state_ref[...]


def ssd_scan(xbc, z, dt_dir, dtT_dir, bT, a_log, d_skip_row, norm_g, h0):
    b, l, _ = xbc.shape
    nc = l // CHUNK
    q = CHUNK
    has_h0 = h0 is not None
    ce = lambda d, c: jnp.where(d == 0, c, nc - 1 - c)
    late = lambda d, c: jnp.where(d == 0, nc - 1, nc - 1 - c)
    in_specs = [
        pl.BlockSpec((1, q, CONV_DIM), lambda i, d, c: (i, ce(d, c), 0)),
        pl.BlockSpec((1, q, D_INNER), lambda i, d, c: (i, late(d, c), 0)),
        pl.BlockSpec((1, 1, q, SSM_H), lambda i, d, c: (d, i, ce(d, c), 0)),
        pl.BlockSpec((1, 1, SSM_H, q), lambda i, d, c: (d, i, 0, ce(d, c))),
        pl.BlockSpec((1, SSM_G * SSM_N, q), lambda i, d, c: (i, 0, ce(d, c))),
        pl.BlockSpec((1, 1, SSM_H), lambda i, d, c: (d, 0, 0)),
        pl.BlockSpec((1, SSM_H, 1), lambda i, d, c: (d, 0, 0)),
        pl.BlockSpec((1, D_INNER), lambda i, d, c: (0, 0)),
        pl.BlockSpec((1, D_INNER), lambda i, d, c: (0, 0)),
    ]
    args = [xbc, z, dt_dir, dtT_dir, bT, a_log.reshape(2, 1, SSM_H), a_log.reshape(2, SSM_H, 1), d_skip_row, norm_g]
    st_spec = pl.BlockSpec((1, 1, SSM_N, D_INNER), lambda i, d, c: (i, d, 0, 0))
    if has_h0:
        in_specs.append(st_spec)
        args.append(h0)
    return pl.pallas_call(
        functools.partial(_ssd_kernel, nc=nc, has_h0=has_h0), grid=(b, 2, nc),
        in_specs=in_specs,
        out_specs=[pl.BlockSpec((1, q, D_INNER), lambda i, d, c: (i, late(d, c), 0)), st_spec],
        out_shape=[jax.ShapeDtypeStruct((b, l, D_INNER), bf16), jax.ShapeDtypeStruct((b, 2, SSM_N, D_INNER), f32)],
        scratch_shapes=[
            pltpu.VMEM((SSM_N, D_INNER), f32),
            pltpu.VMEM((l, D_INNER), f32),
            pltpu.VMEM((q, D_INNER), bf16),
            pltpu.VMEM((q, D_INNER), f32),
            pltpu.VMEM((q, GROUP_W), f32),
            pltpu.VMEM((8, D_INNER), f32),
            pltpu.VMEM((3, q, SSM_H), f32),
            pltpu.VMEM((2, SSM_H, q), f32),
        ],
        compiler_params=_cparams("arbitrary", "arbitrary", "arbitrary"), name="ssd_scan",
    )(*args)


def _router_kernel(x_ref, g_ref, mod_ref, wr_ref, h_ref, aff_ref):
    h = _norm_mod(x_ref[...], g_ref[...], mod_ref[0], SHIFT2, SCALE2)
    hb = h.astype(bf16)
    h_ref[...] = hb
    h_lo = (h - hb.astype(f32)).astype(bf16)
    w = wr_ref[...]
    w_hi = w.astype(bf16)
    w_lo = (w - w_hi.astype(f32)).astype(bf16)
    lg = _dot_nt(w_hi, hb) + _dot_nt(w_hi, h_lo) + _dot_nt(w_lo, hb)
    e = jnp.exp(lg - jnp.max(lg, axis=0, keepdims=True))
    aff_ref[...] = e / jnp.sum(e, axis=0, keepdims=True)


def moe_router(x, g, mod, w_router_t, rows_per_mod, tm=512):
    t = x.shape[0]
    return pl.pallas_call(
        _router_kernel, grid=(t // tm,),
        in_specs=[pl.BlockSpec((tm, D), lambda i: (i, 0)), pl.BlockSpec((1, D), lambda i: (0, 0)),
                  pl.BlockSpec((1, N_MOD, D), lambda i: ((i * tm) // rows_per_mod, 0, 0)),
                  pl.BlockSpec((N_EXPERTS, D), lambda i: (0, 0))],
        out_specs=[pl.BlockSpec((tm, D), lambda i: (i, 0)), pl.BlockSpec((N_EXPERTS, tm), lambda i: (0, i))],
        out_shape=[jax.ShapeDtypeStruct((t, D), bf16), jax.ShapeDtypeStruct((N_EXPERTS, t), f32)],
        compiler_params=_cparams("arbitrary"), name="moe_router",
    )(x, g, mod, w_router_t)


def _route_kernel(aff_ref, key_ref, *, cap):
    t = aff_ref.shape[1]
    bits = lax.bitcast_convert_type(aff_ref[...], i32)
    thr = jnp.zeros((N_EXPERTS, 1), i32)
    for bit in range(30, -1, -1):
        cand = thr | (1 << bit)
        cnt = jnp.sum((bits >= cand).astype(i32), axis=1, keepdims=True)
        thr = jnp.where(cnt >= cap, cand, thr)
    gt = bits > thr
    eq = bits == thr
    gtf = jnp.where(gt, 1.0, 0.0)
    eqf = jnp.where(eq, 1.0, 0.0)
    need = cap - jnp.sum(gtf, axis=1, keepdims=True).astype(i32)
    blk = 256
    tri = (lax.broadcasted_iota(i32, (blk, blk), 0) <= lax.broadcasted_iota(i32, (blk, blk), 1))
    tri = jnp.where(tri, 1.0, 0.0).astype(bf16)
    carry = jnp.zeros((2 * N_EXPERTS, 1), f32)
    for j in range(t // blk):
        sl = slice(j * blk, (j + 1) * blk)
        m = jnp.concatenate([gtf[:, sl], eqf[:, sl]], axis=0)
        pc = _dot(m.astype(bf16), tri) + carry
        carry = pc[:, blk - 1:blk]
        cs_gt = pc[:N_EXPERTS].astype(i32)
        cs_eq = pc[N_EXPERTS:].astype(i32)
        sel = gt[:, sl] | (eq[:, sl] & (cs_eq <= need))
        key_ref[:, sl] = jnp.where(sel, cs_gt + jnp.minimum(cs_eq, need), 0)


def moe_route(aff_t, cap):
    t = aff_t.shape[1]
    return pl.pallas_call(
        functools.partial(_route_kernel, cap=cap), grid=(1,),
        in_specs=[pl.BlockSpec((N_EXPERTS, t), lambda i: (0, 0))],
        out_specs=pl.BlockSpec((N_EXPERTS, t), lambda i: (0, 0)),
        out_shape=jax.ShapeDtypeStruct((N_EXPERTS, t), i32),
        compiler_params=_cparams("arbitrary"), name="moe_route",
    )(aff_t)


def _moe_ffn_kernel(key_ref, aff_ref, h_ref, wg_ref, wu_ref, wd_ref, o_ref, acc_ref, gate_ref, *, nk):
    j = pl.program_id(1)
    k = pl.program_id(2)
    tm = acc_ref.shape[0]

    @pl.when(k == 0)
    def _():
        acc_ref[...] = jnp.zeros_like(acc_ref)
        gate_ref[...] = jnp.zeros_like(gate_ref)

    slot = j * tm + lax.broadcasted_iota(i32, (tm, 1), 0) + 1
    oh = key_ref[0] == slot
    acc_ref[...] += _dot(jnp.where(oh, 1.0, 0.0).astype(bf16), h_ref[...])
    gate_ref[...] += jnp.sum(jnp.where(oh, aff_ref[0], 0.0), axis=1, keepdims=True)

    @pl.when(k == nk - 1)
    def _():
        xe = acc_ref[...].astype(bf16)
        hid = _silu(_dot(xe, wg_ref[0].astype(bf16))) * _dot(xe, wu_ref[0].astype(bf16))
        o_ref[0] = _dot(hid.astype(bf16), wd_ref[0].astype(bf16)) * gate_ref[...]


def moe_ffn(key, aff_t, h, wg, wu, wd, cap, tm=512, tb=1024):
    t = h.shape[0]
    nk = t // tb
    key3 = key.reshape(N_EXPERTS, 1, t)
    aff3 = aff_t.reshape(N_EXPERTS, 1, t)
    wspec = pl.BlockSpec((1, D, D_FF), lambda e, j, k: (e, 0, 0))
    return pl.pallas_call(
        functools.partial(_moe_ffn_kernel, nk=nk), grid=(N_EXPERTS, cap // tm, nk),
        in_specs=[pl.BlockSpec((1, 1, tb), lambda e, j, k: (e, 0, k)), pl.BlockSpec((1, 1, tb), lambda e, j, k: (e, 0, k)),
                  pl.BlockSpec((tb, D), lambda e, j, k: (k, 0)), wspec, wspec,
                  pl.BlockSpec((1, D_FF, D), lambda e, j, k: (e, 0, 0))],
        out_specs=pl.BlockSpec((1, tm, D), lambda e, j, k: (e, j, 0)),
        out_shape=jax.ShapeDtypeStruct((N_EXPERTS, cap, D), f32),
        scratch_shapes=[pltpu.VMEM((tm, D), f32), pltpu.VMEM((tm, 1), f32)],
        compiler_params=_cparams("arbitrary", "arbitrary", "arbitrary"), name="moe_ffn",
    )(key3, aff3, h, wg, wu, wd)


def _moe_combine_kernel(key_ref, ye_ref, x_ref, mod_ref, o_ref, acc_ref, *, ne, nj):
    e = pl.program_id(1)
    j = pl.program_id(2)
    tm = ye_ref.shape[1]

    @pl.when((e == 0) & (j == 0))
    def _():
        acc_ref[...] = jnp.zeros_like(acc_ref)

    slot = j * tm + lax.broadcasted_iota(i32, (tm, 1), 0) + 1
    oh = jnp.where(key_ref[0] == slot, 1.0, 0.0).astype(bf16)
    ye = ye_ref[0]
    hi = ye.astype(bf16)
    lo = (ye - hi.astype(f32)).astype(bf16)
    acc_ref[...] += _dot_tn(oh, hi) + _dot_tn(oh, lo)

    @pl.when((e == ne - 1) & (j == nj - 1))
    def _():
        o_ref[...] = x_ref[...] + mod_ref[0][GATE2:GATE2 + 1, :] * acc_ref[...]


def moe_combine(key, ye, x, mod, cap, rows_per_mod, tm=512, tb=512):
    t = x.shape[0]
    nj = cap // tm
    key3 = key.reshape(N_EXPERTS, 1, t)
    return pl.pallas_call(
        functools.partial(_moe_combine_kernel, ne=N_EXPERTS, nj=nj), grid=(t // tb, N_EXPERTS, nj),
        in_specs=[pl.BlockSpec((1, 1, tb), lambda i, e, j: (e, 0, i)),
                  pl.BlockSpec((1, tm, D), lambda i, e, j: (e, j, 0)),
                  pl.BlockSpec((tb, D), lambda i, e, j: (i, 0)),
                  pl.BlockSpec((1, N_MOD, D), lambda i, e, j: ((i * tb) // rows_per_mod, 0, 0))],
        out_specs=pl.BlockSpec((tb, D), lambda i, e, j: (i, 0)),
        out_shape=jax.ShapeDtypeStruct((t, D), f32),
        scratch_shapes=[pltpu.VMEM((tb, D), f32)],
        compiler_params=_cparams("arbitrary", "arbitrary", "arbitrary"), name="moe_combine",
    )(key3, ye, x, mod)


def moe_layer(x, g2, mod, w_router_t, wg, wu, wd, rows_per_mod):
    t = x.shape[0]
    cap = 2 * t // N_EXPERTS
    h, aff_t = moe_router(x, g2, mod, w_router_t, rows_per_mod)
    key = moe_route(aff_t, cap)
    ye = moe_ffn(key, aff_t, h, wg, wu, wd, cap)
    return moe_combine(key, ye, x, mod, cap, rows_per_mod)


def _final_norm_kernel(x_ref, g_ref, o_ref):
    o_ref[...] = _rms(x_ref[...], g_ref[...])


def final_norm(x, g, tm=1024):
    t = x.shape[0]
    return pl.pallas_call(
        _final_norm_kernel, grid=(t // tm,),
        in_specs=[pl.BlockSpec((tm, D), lambda i: (i, 0)), pl.BlockSpec((1, D), lambda i: (0, 0))],
        out_specs=pl.BlockSpec((tm, D), lambda i: (i, 0)),
        out_shape=jax.ShapeDtypeStruct((t, D), f32),
        compiler_params=_cparams("arbitrary"), name="final_norm",
    )(x, g)


def _rope_tables(seq):
    rows = seq // GRID_W
    row = jnp.repeat(jnp.arange(rows), GRID_W).astype(f32)
    col = jnp.tile(jnp.arange(GRID_W), rows).astype(f32)
    pairs = QK_ROPE // 4
    inv = ROPE_THETA ** (-jnp.arange(pairs, dtype=f32) / pairs)
    ang = jnp.concatenate([row[:, None] * inv, col[:, None] * inv], axis=-1)
    cos, sin = jnp.cos(ang), jnp.sin(ang)
    zero = jnp.zeros_like(cos)
    c = jnp.concatenate([cos, cos, zero, zero], axis=-1)
    s1 = jnp.concatenate([-sin, zero, zero, zero], axis=-1)
    s2 = jnp.concatenate([zero, sin, zero, zero], axis=-1)
    return c, s1, s2


def _mla_weights(w_dq, q_norm_g, w_uq, w_dkv, kv_norm_g, w_ukv, w_o):
    per_head = QK_NOPE + QK_ROPE
    uq = w_uq.reshape(Q_LORA, MLA_HEADS, per_head)
    uq_nope = uq[..., :QK_NOPE].reshape(Q_LORA, MLA_HEADS * QK_NOPE)
    uq_rope = jnp.pad(uq[..., QK_NOPE:], ((0, 0), (0, 0), (0, HEAD_PAD - QK_ROPE))).reshape(Q_LORA, MLA_HEADS * HEAD_PAD)
    wuq = jnp.concatenate([uq_nope, uq_rope], axis=1).astype(bf16)
    wdkv = jnp.pad(w_dkv, ((0, 0), (0, HEAD_PAD - QK_ROPE))).astype(bf16)
    ukv = w_ukv.reshape(KV_LORA, MLA_HEADS, QK_NOPE + V_HEAD)
    wukv = jnp.concatenate([ukv[..., :QK_NOPE].reshape(KV_LORA, -1), ukv[..., QK_NOPE:].reshape(KV_LORA, -1)], axis=1).astype(bf16)
    proj = (w_dq.astype(bf16), q_norm_g.reshape(1, Q_LORA), wuq, wdkv, kv_norm_g.reshape(1, KV_LORA), wukv)
    return proj, wukv, w_o.astype(bf16)


def _mla_layer(xp, xs, modp, mods, g1, cache_ckv_j, cache_krope_j, w, rope_tabs):
    proj_w, wukv, w_o = w
    bp, bs = xp.shape[0] // 256, xs.shape[0] // 2048
    nh = MLA_HEADS * HEAD_PAD
    qn, qr, kn, v, kr, ckv, krope = mla_project(xp, g1, modp, proj_w, None, xp.shape[0])
    r3 = lambda a, b: a.reshape(b, -1, a.shape[-1])
    op = mla_attention(r3(qn, bp), r3(qr, bp), r3(kn, bp), r3(kr, bp), r3(v, bp)).reshape(-1, nh)
    xp = matmul_residual(op, w_o, xp, modp, GATE1, xp.shape[0])
    new_ckv = ckv.reshape(bp, -1, KV_LORA)
    new_krope = krope[:, :QK_ROPE].reshape(bp, -1, QK_ROPE)
    qn, qr, kn, v, kr, _, _ = mla_project(xs, g1, mods, proj_w, rope_tabs, 2048)
    ctx = matmul_bf16(cache_ckv_j.reshape(-1, KV_LORA).astype(bf16), wukv, tm=512)
    nk = MLA_HEADS * QK_NOPE
    kn_all = jnp.concatenate([ctx[:, :nk].reshape(bs, -1, nk), r3(kn, bs)], axis=1)
    v_all = jnp.concatenate([ctx[:, nk:].reshape(bs, -1, nk), r3(v, bs)], axis=1)
    kr_ctx = jnp.pad(cache_krope_j, ((0, 0), (0, 0), (0, HEAD_PAD - QK_ROPE))).astype(bf16)
    kr_all = jnp.concatenate([kr_ctx, r3(kr, bs)], axis=1)
    os_ = mla_attention(r3(qn, bs), r3(qr, bs), kn_all, kr_all, v_all).reshape(-1, nh)
    xs = matmul_residual(os_, w_o, xs, mods, GATE1, 2048)
    return xp, xs, new_ckv, new_krope


def _ssm_stream(x, mod, g1, w, rows_per_mod, seq, h0):
    w_z, w_xbc, w_dt, conv_wt, conv_b, dt_bias, a_log, d_skip_row, norm_g, w_out = w
    b = x.shape[0] // seq
    z = ssm_in_proj(x, g1, mod, w_z, rows_per_mod, "plain")
    xbc = ssm_in_proj(x, g1, mod, w_xbc, rows_per_mod, "conv", period=seq, conv_w=conv_wt, bias=conv_b)
    dt = ssm_in_proj(x, g1, mod, w_dt, rows_per_mod, "dt", bias=dt_bias)
    dt4 = dt[:, :2 * SSM_H].reshape(b, seq, 2, SSM_H)
    dt_dir = dt4.transpose(2, 0, 1, 3)
    dtT_dir = dt4.transpose(2, 0, 3, 1)
    xbc3 = xbc.reshape(b, seq, CONV_DIM)
    bT = xbc3[:, :, D_INNER:D_INNER + SSM_G * SSM_N].transpose(0, 2, 1)
    y, hout = ssd_scan(xbc3, z.reshape(b, seq, D_INNER), dt_dir, dtT_dir, bT, a_log, d_skip_row, norm_g, h0)
    x = matmul_residual(y.reshape(-1, D_INNER), w_out, x, mod, GATE1, rows_per_mod)
    new_state = hout.reshape(b, 2, SSM_N, SSM_H, SSM_P).transpose(0, 1, 3, 4, 2)
    return x, new_state


def kernel(x_prompt, x_sample, cache_ckv, cache_krope, state_ssm, c, c_ctx, w_mod, b_mod, norm1_g, norm2_g, final_norm_g, mla_w_dq, mla_q_norm_g, mla_w_uq, mla_w_dkv, mla_kv_norm_g, mla_w_ukv, mla_w_o, ssm_w_in, ssm_conv_w, ssm_conv_b, ssm_dt_bias, ssm_a_log, ssm_d_skip, ssm_norm_g, ssm_w_out, moe_w_router, moe_w_gate, moe_w_up, moe_w_down):
    bp, lp, _ = x_prompt.shape
    bs, ls, _ = x_sample.shape
    xp = x_prompt.reshape(bp * lp, D)
    xs = x_sample.reshape(bs * ls, D)

    cond8 = jnp.concatenate([c_ctx[None, :], c, jnp.zeros((8 - 1 - bs, D), f32)], axis=0)
    mod_all = modulation_all(cond8, w_mod, b_mod)
    rope_tabs = _rope_tables(ls)

    new_ckv, new_krope, new_ssm = [], [], []
    for l in range(DEPTH):
        modp = mod_all[l, 0:1].reshape(1, N_MOD, D)
        mods = mod_all[l, 1:1 + bs].reshape(bs, N_MOD, D)
        g1 = norm1_g[l].reshape(1, D)
        j = l // 2
        if l % 2 == 0:
            w = _mla_weights(mla_w_dq[j], mla_q_norm_g[j], mla_w_uq[j], mla_w_dkv[j], mla_kv_norm_g[j], mla_w_ukv[j], mla_w_o[j])
            xp, xs, ckv, krope = _mla_layer(xp, xs, modp, mods, g1, cache_ckv[:, j], cache_krope[:, j], w, rope_tabs)
            new_ckv.append(ckv)
            new_krope.append(krope)
        else:
            w_in = ssm_w_in[j]
            w_dt = jnp.pad(w_in[:, D_INNER + CONV_DIM:], ((0, 0), (0, 128 - 2 * SSM_H))).astype(bf16)
            dt_bias = jnp.pad(ssm_dt_bias[j].reshape(1, 2 * SSM_H), ((0, 0), (0, 128 - 2 * SSM_H)))
            w = (w_in[:, :D_INNER].astype(bf16), w_in[:, D_INNER:D_INNER + CONV_DIM].astype(bf16), w_dt,
                 ssm_conv_w[j].T, ssm_conv_b[j].reshape(1, CONV_DIM), dt_bias, ssm_a_log[j],
                 jnp.repeat(ssm_d_skip[j], SSM_P).reshape(1, D_INNER), ssm_norm_g[j].reshape(1, D_INNER),
                 ssm_w_out[j].astype(bf16))
            xp, st_p = _ssm_stream(xp, modp, g1, w, bp * lp, lp, None)
            h0 = state_ssm[:, j].transpose(0, 1, 4, 2, 3).reshape(bs, 2, SSM_N, D_INNER)
            xs, _ = _ssm_stream(xs, mods, g1, w, ls, ls, h0)
            new_ssm.append(st_p)
        g2 = norm2_g[l].reshape(1, D)
        moe_w = (moe_w_router[l].T, moe_w_gate[l], moe_w_up[l], moe_w_down[l])
        xp = moe_layer(xp, g2, modp, *moe_w, bp * lp)
        xs = moe_layer(xs, g2, mods, *moe_w, ls)

    fg = final_norm_g.reshape(1, D)
    y_prompt = final_norm(xp, fg).reshape(bp, lp, D)
    y_sample = final_norm(xs, fg).reshape(bs, ls, D)
    return (y_prompt, y_sample, jnp.stack(new_ckv, axis=1), jnp.stack(new_krope, axis=1), jnp.stack(new_ssm, axis=1))
```

```python
import functools

import jax
import jax.numpy as jnp
from jax import lax
from jax.experimental import pallas as pl
from jax.experimental.pallas import tpu as pltpu

f32 = jnp.float32
bf16 = jnp.bfloat16
i32 = jnp.int32

D = 1024
DEPTH = 4
N_MOD = 6
EPS = 1e-6
GRID_W = 64

MLA_HEADS = 8
QK_NOPE = 128
QK_ROPE = 64
V_HEAD = 128
Q_LORA = 384
KV_LORA = 256
ROPE_THETA = 10000.0
ATTN_SCALE = (QK_NOPE + QK_ROPE) ** -0.5
HEAD_PAD = 128

D_INNER = 2 * D
SSM_P = 64
SSM_H = D_INNER // SSM_P
SSM_G = 4
SSM_N = 128
CHUNK = 128
CONV_DIM = D_INNER + 2 * SSM_G * SSM_N
HEADS_PER_GROUP = SSM_H // SSM_G
GROUP_W = HEADS_PER_GROUP * SSM_P

N_EXPERTS = 16
D_FF = 1024
TOKEN_BLOCK = 256
SLOT_TILE = 256
COMBINE_WINDOW = 64

VMEM_LIMIT = 56 * 1024 * 1024

SHIFT1, SCALE1, GATE1, SHIFT2, SCALE2, GATE2 = range(6)


def _cparams(*sem):
    return pltpu.CompilerParams(dimension_semantics=sem, vmem_limit_bytes=VMEM_LIMIT)


def _dot(a, b):
    return jnp.dot(a, b, preferred_element_type=f32)


def _dot_nt(a, b):
    return lax.dot_general(a, b, (((1,), (1,)), ((), ())), preferred_element_type=f32)


def _dot_tn(a, b):
    return lax.dot_general(a, b, (((0,), (0,)), ((), ())), preferred_element_type=f32)


def _dot_f32(a, b):
    return jnp.dot(a, b, preferred_element_type=f32, precision=lax.Precision.HIGHEST)


def _rms(x, g):
    ms = jnp.mean(x * x, axis=-1, keepdims=True)
    return x * lax.rsqrt(ms + EPS) * g


def _norm_mod(x, g, mod, k_shift, k_scale):
    return _rms(x, g) * (1.0 + mod[k_scale:k_scale + 1, :]) + mod[k_shift:k_shift + 1, :]


def _silu(x):
    return x * jax.nn.sigmoid(x)


def _mod_kernel(c_ref, w_ref, b_ref, o_ref):
    s = _silu(c_ref[...]).astype(bf16)
    o_ref[0] = _dot(s, w_ref[0].astype(bf16)) + b_ref[0]


def modulation_all(cond8, w_mod, b_mod):
    nb = 1536
    n = N_MOD * D
    return pl.pallas_call(
        _mod_kernel,
        grid=(DEPTH, n // nb),
        in_specs=[
            pl.BlockSpec((8, D), lambda l, j: (0, 0)),
            pl.BlockSpec((1, D, nb), lambda l, j: (l, 0, j)),
            pl.BlockSpec((1, 1, nb), lambda l, j: (l, 0, j)),
        ],
        out_specs=pl.BlockSpec((1, 8, nb), lambda l, j: (l, 0, j)),
        out_shape=jax.ShapeDtypeStruct((DEPTH, 8, n), f32),
        compiler_params=_cparams("arbitrary", "arbitrary"),
        name="modulation",
    )(cond8, w_mod, b_mod.reshape(DEPTH, 1, n))


def _rope_rot(p, c, s1, s2):
    return p * c + pltpu.roll(p, 96, axis=1) * s1 + pltpu.roll(p, 32, axis=1) * s2


def _mla_proj_kernel(*refs, rope):
    if rope:
        (x_ref, g_ref, mod_ref, wdq_ref, qg_ref, wuq_ref, wdkv_ref, kvg_ref, wukv_ref, rc_ref, rs1_ref, rs2_ref,
         qn_ref, qr_ref, kn_ref, v_ref, kr_ref, ckv_ref, krope_ref) = refs
    else:
        (x_ref, g_ref, mod_ref, wdq_ref, qg_ref, wuq_ref, wdkv_ref, kvg_ref, wukv_ref,
         qn_ref, qr_ref, kn_ref, v_ref, kr_ref, ckv_ref, krope_ref) = refs
    h = _norm_mod(x_ref[...], g_ref[...], mod_ref[0], SHIFT1, SCALE1).astype(bf16)
    nq = MLA_HEADS * QK_NOPE
    q_lat = _rms(_dot(h, wdq_ref[...]), qg_ref[...]).astype(bf16)
    q = _dot(q_lat, wuq_ref[...])
    qn_ref[...] = q[:, :nq].astype(bf16)
    if rope:
        c, s1, s2 = rc_ref[...], rs1_ref[...], rs2_ref[...]
    for hh in range(MLA_HEADS):
        piece = q[:, nq + hh * HEAD_PAD: nq + (hh + 1) * HEAD_PAD]
        if rope:
            piece = _rope_rot(piece, c, s1, s2)
        qr_ref[:, hh * HEAD_PAD:(hh + 1) * HEAD_PAD] = piece.astype(bf16)
    kv = _dot(h, wdkv_ref[...])
    ckv = _rms(kv[:, :KV_LORA], kvg_ref[...])
    ckv_ref[...] = ckv
    kr = kv[:, KV_LORA:]
    krope_ref[...] = kr
    if rope:
        kr = _rope_rot(kr, c, s1, s2)
    kr_ref[...] = kr.astype(bf16)
    kvx = _dot(ckv.astype(bf16), wukv_ref[...])
    nk = MLA_HEADS * QK_NOPE
    kn_ref[...] = kvx[:, :nk].astype(bf16)
    v_ref[...] = kvx[:, nk:].astype(bf16)


def mla_project(x, g, mod, w, rope_tabs, rows_per_mod, tm=256):
    t = x.shape[0]
    wdq, qg, wuq, wdkv, kvg, wukv = w
    full = lambda a: pl.BlockSpec(a.shape, lambda i: (0,) * a.ndim)
    row = lambda n: pl.BlockSpec((tm, n), lambda i: (i, 0))
    in_specs = [row(D), full(g), pl.BlockSpec((1, N_MOD, D), lambda i: ((i * tm) // rows_per_mod, 0, 0)),
                full(wdq), full(qg), full(wuq), full(wdkv), full(kvg), full(wukv)]
    args = [x, g, mod, wdq, qg, wuq, wdkv, kvg, wukv]
    rope = rope_tabs is not None
    if rope:
        nrb = rope_tabs[0].shape[0] // tm
        in_specs += [pl.BlockSpec((tm, HEAD_PAD), lambda i: (i % nrb, 0))] * 3
        args += list(rope_tabs)
    nh = MLA_HEADS * HEAD_PAD
    out_shape = [jax.ShapeDtypeStruct((t, nh), bf16)] * 4 + [
        jax.ShapeDtypeStruct((t, HEAD_PAD), bf16), jax.ShapeDtypeStruct((t, KV_LORA), f32),
        jax.ShapeDtypeStruct((t, HEAD_PAD), f32)]
    out_specs = [row(nh)] * 4 + [row(HEAD_PAD), row(KV_LORA), row(HEAD_PAD)]
    return pl.pallas_call(
        functools.partial(_mla_proj_kernel, rope=rope),
        grid=(t // tm,), in_specs=in_specs, out_specs=out_specs, out_shape=out_shape,
        compiler_params=_cparams("arbitrary"), name="mla_project",
    )(*args)


def _mm_kernel(a_ref, w_ref, o_ref):
    o_ref[...] = _dot(a_ref[...], w_ref[...]).astype(o_ref.dtype)


def matmul_bf16(a, w, tm):
    m, k = a.shape
    n = w.shape[1]
    return pl.pallas_call(
        _mm_kernel, grid=(m // tm,),
        in_specs=[pl.BlockSpec((tm, k), lambda i: (i, 0)), pl.BlockSpec((k, n), lambda i: (0, 0))],
        out_specs=pl.BlockSpec((tm, n), lambda i: (i, 0)),
        out_shape=jax.ShapeDtypeStruct((m, n), bf16),
        compiler_params=_cparams("arbitrary"), name="matmul_bf16",
    )(a, w)


def _attn_kernel(qn_ref, qr_ref, kn_ref, kr_ref, v_ref, o_ref):
    kr = kr_ref[0]
    for h in range(MLA_HEADS):
        sl = slice(h * HEAD_PAD, (h + 1) * HEAD_PAD)
        s = (_dot_nt(qn_ref[0, :, sl], kn_ref[0, :, sl]) + _dot_nt(qr_ref[0, :, sl], kr)) * ATTN_SCALE
        e = jnp.exp(s - jnp.max(s, axis=-1, keepdims=True))
        l = jnp.sum(e, axis=-1, keepdims=True)
        o = _dot(e.astype(bf16), v_ref[0, :, sl]) / l
        o_ref[0, :, sl] = o.astype(bf16)


def mla_attention(qn, qr, kn, kr, v, tq=256):
    b, lq, nh = qn.shape
    s = kn.shape[1]
    qspec = pl.BlockSpec((1, tq, nh), lambda i, j: (i, j, 0))
    kspec = pl.BlockSpec((1, s, nh), lambda i, j: (i, 0, 0))
    return pl.pallas_call(
        _attn_kernel, grid=(b, lq // tq),
        in_specs=[qspec, qspec, kspec, pl.BlockSpec((1, s, HEAD_PAD), lambda i, j: (i, 0, 0)), kspec],
        out_specs=qspec, out_shape=jax.ShapeDtypeStruct((b, lq, nh), bf16),
        compiler_params=_cparams("arbitrary", "arbitrary"), name="mla_attention",
    )(qn, qr, kn, kr, v)


def _mm_res_kernel(a_ref, w_ref, x_ref, mod_ref, o_ref, *, kgate):
    y = _dot(a_ref[...], w_ref[...])
    o_ref[...] = x_ref[...] + mod_ref[0][kgate:kgate + 1, :] * y


def matmul_residual(a, w, x, mod, kgate, rows_per_mod, tm=512):
    t, k = a.shape
    return pl.pallas_call(
        functools.partial(_mm_res_kernel, kgate=kgate), grid=(t // tm,),
        in_specs=[pl.BlockSpec((tm, k), lambda i: (i, 0)), pl.BlockSpec((k, D), lambda i: (0, 0)),
                  pl.BlockSpec((tm, D), lambda i: (i, 0)),
                  pl.BlockSpec((1, N_MOD, D), lambda i: ((i * tm) // rows_per_mod, 0, 0))],
        out_specs=pl.BlockSpec((tm, D), lambda i: (i, 0)),
        out_shape=jax.ShapeDtypeStruct((t, D), f32),
        compiler_params=_cparams("arbitrary"), name="matmul_residual",
    )(a, w, x, mod)


INPROJ_BLOCK = 512
INPROJ_Z_BLOCKS = D_INNER // INPROJ_BLOCK
INPROJ_CONV_BLOCKS = CONV_DIM // INPROJ_BLOCK
DT_PAD = 128


def _inproj_kernel(x_ref, g_ref, mod_ref, w_ref, cw_ref, b_ref, z_ref, xbc_ref, dt_ref, h_ref, *, period):
    j = pl.program_id(1)

    @pl.when(j == 0)
    def _():
        h_ref[...] = _norm_mod(x_ref[...], g_ref[...], mod_ref[0], SHIFT1, SCALE1).astype(bf16)

    @pl.when(j < INPROJ_Z_BLOCKS)
    def _():
        z_ref[...] = _dot(h_ref[...], w_ref[...])

    @pl.when((j >= INPROJ_Z_BLOCKS) & (j < INPROJ_Z_BLOCKS + INPROJ_CONV_BLOCKS))
    def _():
        y = _dot(h_ref[...], w_ref[...])
        tm = y.shape[0]
        pos = lax.broadcasted_iota(i32, (tm, 1), 0) & (period - 1)
        prev = jnp.where(pos == 0, 0.0, pltpu.roll(y, 1, axis=0))
        nxt = jnp.where(pos == period - 1, 0.0, pltpu.roll(y, tm - 1, axis=0))
        cw = cw_ref[...]
        xbc_ref[...] = _silu(cw[0:1, :] * prev + cw[1:2, :] * y + cw[2:3, :] * nxt + b_ref[...])

    @pl.when(j == INPROJ_Z_BLOCKS + INPROJ_CONV_BLOCKS)
    def _():
        y = _dot(h_ref[...], w_ref[:, :DT_PAD]) + b_ref[:, :DT_PAD]
        dt_ref[...] = jnp.maximum(y, 0.0) + jnp.log1p(jnp.exp(-jnp.abs(y)))


def ssm_in_proj(x, g, mod, w_all, conv_w_all, bias_all, rows_per_mod, period, tm=2048):
    t = x.shape[0]
    nb = INPROJ_BLOCK
    nz, nx = INPROJ_Z_BLOCKS, INPROJ_CONV_BLOCKS
    col = lambda rows: pl.BlockSpec((rows, nb), lambda i, j: (0, j))
    return pl.pallas_call(
        functools.partial(_inproj_kernel, period=period), grid=(t // tm, nz + nx + 1),
        in_specs=[pl.BlockSpec((tm, D), lambda i, j: (i, 0)), pl.BlockSpec((1, D), lambda i, j: (0, 0)),
                  pl.BlockSpec((1, N_MOD, D), lambda i, j: ((i * tm) // rows_per_mod, 0, 0)),
                  col(D), col(3), col(1)],
        out_specs=[pl.BlockSpec((tm, nb), lambda i, j: (i, jnp.minimum(j, nz - 1))),
                   pl.BlockSpec((tm, nb), lambda i, j: (i, jnp.clip(j - nz, 0, nx - 1))),
                   pl.BlockSpec((tm, DT_PAD), lambda i, j: (i, 0))],
        out_shape=[jax.ShapeDtypeStruct((t, D_INNER), f32), jax.ShapeDtypeStruct((t, CONV_DIM), f32),
                   jax.ShapeDtypeStruct((t, DT_PAD), f32)],
        scratch_shapes=[pltpu.VMEM((tm, D), bf16)],
        compiler_params=_cparams("arbitrary", "arbitrary"), name="ssm_in_proj",
    )(x, g, mod, w_all, conv_w_all, bias_all)


def _ssd_kernel(*refs, nc, has_h0):
    if has_h0:
        (xbc_ref, z_ref, dt_ref, dtT_ref, bT_ref, alr_ref, alc_ref, dsk_ref, ng_ref, h0_ref,
         y_ref, hout_ref, state_ref, yf_ref, yc_ref, col_ref, row_ref) = refs
    else:
        (xbc_ref, z_ref, dt_ref, dtT_ref, bT_ref, alr_ref, alc_ref, dsk_ref, ng_ref,
         y_ref, hout_ref, state_ref, yf_ref, yc_ref, col_ref, row_ref) = refs
    q = CHUNK
    d = pl.program_id(1)
    c = pl.program_id(2)
    ce = jnp.where(d == 0, c, nc - 1 - c)

    @pl.when(c == 0)
    def _():
        if has_h0:
            state_ref[...] = h0_ref[0, 0]
        else:
            state_ref[...] = jnp.zeros_like(state_ref)

    ii = lax.broadcasted_iota(i32, (q, q), 0)
    jj = lax.broadcasted_iota(i32, (q, q), 1)
    ahead = (ii - jj) * jnp.where(d == 0, 1, -1)
    causal = ahead >= 0
    first_head = lax.broadcasted_iota(i32, (1, 2 * SSM_P), 1) < SSM_P
    m_col = jnp.where(causal, 1.0, 0.0).astype(f32)
    m_row = jnp.where(ahead <= 0, 1.0, 0.0).astype(f32)

    a_row = -jnp.exp(alr_ref[0])
    a_col = -jnp.exp(alc_ref[0])
    dt = dt_ref[0, 0]
    dtT = dtT_ref[0, 0]
    daT = dtT * a_col
    cs_col = _dot_f32(m_col, dt * a_row)
    cs_row = _dot_f32(daT, m_row)
    tot = jnp.sum(daT, axis=1, keepdims=True)
    col_ref[0] = cs_col
    col_ref[1] = jnp.exp(cs_col)
    row_ref[0] = cs_row
    row_ref[1] = dtT
    row_ref[2] = jnp.exp(tot - cs_row) * dtT
    row_ref[3] = jnp.broadcast_to(jnp.exp(tot), (SSM_H, q))

    for g in range(SSM_G):
        b_g = xbc_ref[0, :, D_INNER + g * SSM_N: D_INNER + (g + 1) * SSM_N].astype(bf16)
        c_off = D_INNER + SSM_G * SSM_N
        c_g = xbc_ref[0, :, c_off + g * SSM_N: c_off + (g + 1) * SSM_N].astype(bf16)
        cb = _dot_nt(c_g, b_g)
        bT_g = bT_ref[0, g * SSM_N:(g + 1) * SSM_N, :]
        for pr in range(HEADS_PER_GROUP // 2):
            h0 = g * HEADS_PER_GROUP + 2 * pr
            h1 = h0 + 1
            ps = slice(h0 * SSM_P, (h1 + 1) * SSM_P)
            x2 = xbc_ref[0, :, ps]
            x_blk = jnp.concatenate([jnp.where(first_head, x2, 0.0), jnp.where(first_head, 0.0, x2)], axis=0).astype(bf16)
            st = state_ref[:, ps]
            ws, bs = [], []
            for h in (h0, h1):
                seg = col_ref[0, :, h:h + 1] - row_ref[0, h:h + 1, :]
                w = cb * jnp.exp(jnp.where(causal, seg, -jnp.inf)) * row_ref[1, h:h + 1, :]
                ws.append(w.astype(bf16))
                bs.append((bT_g * row_ref[2, h:h + 1, :]).astype(bf16))
            e_cs = jnp.where(first_head, col_ref[1, :, h0:h0 + 1], col_ref[1, :, h1:h1 + 1])
            yc_ref[:, ps] = _dot(jnp.concatenate(ws, axis=1), x_blk) + _dot(c_g, st.astype(bf16)) * e_cs
            dec = jnp.where(first_head, row_ref[3, h0:h0 + 1, :], row_ref[3, h1:h1 + 1, :])
            state_ref[:, ps] = st * dec + _dot(jnp.concatenate(bs, axis=1), x_blk)

    rows = pl.ds(pl.multiple_of(ce * q, q), q)

    @pl.when(d == 0)
    def _():
        yf_ref[rows, :] = yc_ref[...]

    @pl.when(d == 1)
    def _():
        y = yf_ref[rows, :] + yc_ref[...] + dsk_ref[...] * xbc_ref[0, :, :D_INNER]
        y = y * _silu(z_ref[0])
        y_ref[0] = _rms(y, ng_ref[...]).astype(bf16)

    @pl.when(c == nc - 1)
    def _():
        hout_ref[0, 0] = state_ref[...]


def ssd_scan(xbc, z, dt_dir, dtT_dir, bT, a_log, d_skip_row, norm_g, h0):
    b, l, _ = xbc.shape
    nc = l // CHUNK
    q = CHUNK
    has_h0 = h0 is not None
    ce = lambda d, c: jnp.where(d == 0, c, nc - 1 - c)
    late = lambda d, c: jnp.where(d == 0, nc - 1, nc - 1 - c)
    in_specs = [
        pl.BlockSpec((1, q, CONV_DIM), lambda i, d, c: (i, ce(d, c), 0)),
        pl.BlockSpec((1, q, D_INNER), lambda i, d, c: (i, late(d, c), 0)),
        pl.BlockSpec((1, 1, q, SSM_H), lambda i, d, c: (d, i, ce(d, c), 0)),
        pl.BlockSpec((1, 1, SSM_H, q), lambda i, d, c: (d, i, 0, ce(d, c))),
        pl.BlockSpec((1, SSM_G * SSM_N, q), lambda i, d, c: (i, 0, ce(d, c))),
        pl.BlockSpec((1, 1, SSM_H), lambda i, d, c: (d, 0, 0)),
        pl.BlockSpec((1, SSM_H, 1), lambda i, d, c: (d, 0, 0)),
        pl.BlockSpec((1, D_INNER), lambda i, d, c: (0, 0)),
        pl.BlockSpec((1, D_INNER), lambda i, d, c: (0, 0)),
    ]
    args = [xbc, z, dt_dir, dtT_dir, bT, a_log.reshape(2, 1, SSM_H), a_log.reshape(2, SSM_H, 1), d_skip_row, norm_g]
    st_spec = pl.BlockSpec((1, 1, SSM_N, D_INNER), lambda i, d, c: (i, d, 0, 0))
    if has_h0:
        in_specs.append(st_spec)
        args.append(h0)
    return pl.pallas_call(
        functools.partial(_ssd_kernel, nc=nc, has_h0=has_h0), grid=(b, 2, nc),
        in_specs=in_specs,
        out_specs=[pl.BlockSpec((1, q, D_INNER), lambda i, d, c: (i, late(d, c), 0)), st_spec],
        out_shape=[jax.ShapeDtypeStruct((b, l, D_INNER), bf16), jax.ShapeDtypeStruct((b, 2, SSM_N, D_INNER), f32)],
        scratch_shapes=[
            pltpu.VMEM((SSM_N, D_INNER), f32),
            pltpu.VMEM((l, D_INNER), f32),
            pltpu.VMEM((q, D_INNER), f32),
            pltpu.VMEM((2, q, SSM_H), f32),
            pltpu.VMEM((4, SSM_H, q), f32),
        ],
        compiler_params=_cparams("arbitrary", "arbitrary", "arbitrary"), name="ssd_scan",
    )(*args)


def _router_kernel(x_ref, g_ref, mod_ref, wr_ref, h_ref, aff_ref):
    h = _norm_mod(x_ref[...], g_ref[...], mod_ref[0], SHIFT2, SCALE2)
    hb = h.astype(bf16)
    h_ref[...] = hb
    h_lo = (h - hb.astype(f32)).astype(bf16)
    w = wr_ref[...]
    w_hi = w.astype(bf16)
    w_lo = (w - w_hi.astype(f32)).astype(bf16)
    lg = _dot_nt(w_hi, hb) + _dot_nt(w_hi, h_lo) + _dot_nt(w_lo, hb)
    e = jnp.exp(lg - jnp.max(lg, axis=0, keepdims=True))
    aff_ref[...] = e / jnp.sum(e, axis=0, keepdims=True)


def moe_router(x, g, mod, w_router_t, rows_per_mod, tm=512):
    t = x.shape[0]
    return pl.pallas_call(
        _router_kernel, grid=(t // tm,),
        in_specs=[pl.BlockSpec((tm, D), lambda i: (i, 0)), pl.BlockSpec((1, D), lambda i: (0, 0)),
                  pl.BlockSpec((1, N_MOD, D), lambda i: ((i * tm) // rows_per_mod, 0, 0)),
                  pl.BlockSpec((N_EXPERTS, D), lambda i: (0, 0))],
        out_specs=[pl.BlockSpec((tm, D), lambda i: (i, 0)), pl.BlockSpec((N_EXPERTS, tm), lambda i: (0, i))],
        out_shape=[jax.ShapeDtypeStruct((t, D), bf16), jax.ShapeDtypeStruct((N_EXPERTS, t), f32)],
        compiler_params=_cparams("arbitrary"), name="moe_router",
    )(x, g, mod, w_router_t)


def _route_kernel(aff_ref, key_ref, bnd_ref, *, cap):
    t = aff_ref.shape[1]
    bits = lax.bitcast_convert_type(aff_ref[...], i32)
    thr = jnp.zeros((N_EXPERTS, 1), i32)
    for bit in range(30, -1, -1):
        cand = thr | (1 << bit)
        cnt = jnp.sum((bits >= cand).astype(i32), axis=1, keepdims=True)
        thr = jnp.where(cnt >= cap, cand, thr)
    gt = bits > thr
    eq = bits == thr
    gtf = jnp.where(gt, 1.0, 0.0)
    eqf = jnp.where(eq, 1.0, 0.0)
    need = cap - jnp.sum(gtf, axis=1, keepdims=True).astype(i32)
    blk = TOKEN_BLOCK
    tri = (lax.broadcasted_iota(i32, (blk, blk), 0) <= lax.broadcasted_iota(i32, (blk, blk), 1))
    tri = jnp.where(tri, 1.0, 0.0).astype(bf16)
    carry = jnp.zeros((2 * N_EXPERTS, 1), f32)
    bnd_ref[...] = jnp.zeros_like(bnd_ref)
    for j in range(t // blk):
        sl = slice(j * blk, (j + 1) * blk)
        m = jnp.concatenate([gtf[:, sl], eqf[:, sl]], axis=0)
        pc = _dot(m.astype(bf16), tri) + carry
        carry = pc[:, blk - 1:blk]
        cs_gt = pc[:N_EXPERTS].astype(i32)
        cs_eq = pc[N_EXPERTS:].astype(i32)
        sel = gt[:, sl] | (eq[:, sl] & (cs_eq <= need))
        cs = cs_gt + jnp.minimum(cs_eq, need)
        key_ref[:, sl] = jnp.where(sel, cs, 0)
        bnd_ref[:, j:j + 1] = cs[:, blk - 1:blk]


def moe_route(aff_t, cap):
    t = aff_t.shape[1]
    return pl.pallas_call(
        functools.partial(_route_kernel, cap=cap), grid=(1,),
        in_specs=[pl.BlockSpec((N_EXPERTS, t), lambda i: (0, 0))],
        out_specs=[pl.BlockSpec((N_EXPERTS, t), lambda i: (0, 0)), pl.BlockSpec((N_EXPERTS, 128), lambda i: (0, 0))],
        out_shape=[jax.ShapeDtypeStruct((N_EXPERTS, t), i32), jax.ShapeDtypeStruct((N_EXPERTS, 128), i32)],
        compiler_params=_cparams("arbitrary"), name="moe_route",
    )(aff_t)


def _moe_ffn_kernel(lo_ref, hi_ref, key_ref, aff_ref, h_ref, wg_ref, wu_ref, wd_ref, o_ref, acc_ref, gate_ref, *, nj):
    e = pl.program_id(0)
    j = pl.program_id(1)
    tm = acc_ref.shape[0]
    tb = TOKEN_BLOCK
    acc_ref[...] = jnp.zeros_like(acc_ref)
    gate_ref[...] = jnp.zeros_like(gate_ref)
    slot = j * tm + lax.broadcasted_iota(i32, (tm, 1), 0) + 1

    def body(b, carry):
        oh = key_ref[0, b] == slot
        rows = pl.ds(pl.multiple_of(b * tb, tb), tb)
        acc_ref[...] += _dot(jnp.where(oh, 1.0, 0.0).astype(bf16), h_ref[rows, :])
        gate_ref[...] += jnp.sum(jnp.where(oh, aff_ref[0, b], 0.0), axis=1, keepdims=True)
        return carry

    lax.fori_loop(lo_ref[e * nj + j], hi_ref[e * nj + j] + 1, body, 0)
    xe = acc_ref[...].astype(bf16)
    hid = _silu(_dot(xe, wg_ref[0, 0].astype(bf16))) * _dot(xe, wu_ref[0, 0].astype(bf16))
    o_ref[0] = _dot(hid.astype(bf16), wd_ref[0, 0].astype(bf16)) * gate_ref[...]


def moe_ffn(lo, hi, key, aff_t, h, wg, wu, wd, layer, cap, tm=SLOT_TILE):
    t = h.shape[0]
    nblk = t // TOKEN_BLOCK
    nj = cap // tm
    key4 = key.reshape(N_EXPERTS, nblk, 1, TOKEN_BLOCK)
    aff4 = aff_t.reshape(N_EXPERTS, nblk, 1, TOKEN_BLOCK)
    row_spec = pl.BlockSpec((1, nblk, 1, TOKEN_BLOCK), lambda e, j, *_: (e, 0, 0, 0))
    wspec = pl.BlockSpec((1, 1, D, D_FF), lambda e, j, *_: (layer, e, 0, 0))
    grid_spec = pltpu.PrefetchScalarGridSpec(
        num_scalar_prefetch=2, grid=(N_EXPERTS, nj),
        in_specs=[row_spec, row_spec,
                  pl.BlockSpec((t, D), lambda e, j, *_: (0, 0), pipeline_mode=pl.Buffered(1)),
                  wspec, wspec, pl.BlockSpec((1, 1, D_FF, D), lambda e, j, *_: (layer, e, 0, 0))],
        out_specs=pl.BlockSpec((1, tm, D), lambda e, j, *_: (e, j, 0)),
        scratch_shapes=[pltpu.VMEM((tm, D), f32), pltpu.VMEM((tm, 1), f32)])
    return pl.pallas_call(
        functools.partial(_moe_ffn_kernel, nj=nj), grid_spec=grid_spec,
        out_shape=jax.ShapeDtypeStruct((N_EXPERTS, cap, D), f32),
        compiler_params=_cparams("arbitrary", "arbitrary"), name="moe_ffn",
    )(lo, hi, key4, aff4, h, wg, wu, wd)


def _moe_combine_kernel(start_ref, nround_ref, key_ref, ye_hbm, x_ref, mod_ref, o_ref, stage_ref, acc_ref, sem,
                        *, cap, nblk):
    i = pl.program_id(0)
    w = COMBINE_WINDOW

    def window_start(tile, e, r):
        return start_ref[tile * N_EXPERTS + e] + r * w

    def copies(tile, r, buf):
        out = []
        for e in range(N_EXPERTS):
            first = pl.multiple_of(jnp.minimum(window_start(tile, e, r), cap - w), 8)
            out.append(pltpu.make_async_copy(ye_hbm.at[e, pl.ds(first, w), :],
                                             stage_ref.at[buf, pl.ds(e * w, w), :], sem.at[buf]))
        return out

    def accumulate(r, buf):
        pieces = []
        for e in range(N_EXPERTS):
            begin = window_start(i, e, r)
            slot = jnp.minimum(begin, cap - w) + lax.broadcasted_iota(i32, (w, 1), 0) + 1
            oh = (key_ref[e:e + 1, :] == slot) & (slot > begin)
            pieces.append(jnp.where(oh, 1.0, 0.0))
        oh = jnp.concatenate(pieces, axis=0).astype(bf16)
        rows = stage_ref[buf]
        hi = rows.astype(bf16)
        lo = (rows - hi.astype(f32)).astype(bf16)
        acc_ref[...] += _dot_tn(oh, hi) + _dot_tn(oh, lo)

    cur = lax.rem(i, 2)

    @pl.when(i == 0)
    def _():
        for cp in copies(0, 0, 0):
            cp.start()

    @pl.when(i + 1 < nblk)
    def _():
        for cp in copies(i + 1, 0, 1 - cur):
            cp.start()

    acc_ref[...] = jnp.zeros_like(acc_ref)
    for cp in copies(i, 0, cur):
        cp.wait()
    accumulate(0, cur)

    def extra_round(r, carry):
        cps = copies(i, r, 2)
        for cp in cps:
            cp.start()
        for cp in cps:
            cp.wait()
        accumulate(r, 2)
        return carry

    lax.fori_loop(1, nround_ref[i], extra_round, 0)
    o_ref[...] = x_ref[...] + mod_ref[0][GATE2:GATE2 + 1, :] * acc_ref[...]


def moe_combine(starts, nround, key, ye, x, mod, cap, rows_per_mod):
    t = x.shape[0]
    tb = TOKEN_BLOCK
    nblk = t // tb
    grid_spec = pltpu.PrefetchScalarGridSpec(
        num_scalar_prefetch=2, grid=(nblk,),
        in_specs=[pl.BlockSpec((N_EXPERTS, tb), lambda i, *_: (0, i)),
                  pl.BlockSpec(memory_space=pl.ANY),
                  pl.BlockSpec((tb, D), lambda i, *_: (i, 0)),
                  pl.BlockSpec((1, N_MOD, D), lambda i, *_: ((i * tb) // rows_per_mod, 0, 0))],
        out_specs=pl.BlockSpec((tb, D), lambda i, *_: (i, 0)),
        scratch_shapes=[pltpu.VMEM((3, N_EXPERTS * COMBINE_WINDOW, D), f32), pltpu.VMEM((tb, D), f32),
                        pltpu.SemaphoreType.DMA((3,))])
    return pl.pallas_call(
        functools.partial(_moe_combine_kernel, cap=cap, nblk=nblk), grid_spec=grid_spec,
        out_shape=jax.ShapeDtypeStruct((t, D), f32),
        compiler_params=_cparams("arbitrary"), name="moe_combine",
    )(starts, nround, key, ye, x, mod)


def moe_layer(x, g2, mod, w_router_t, wg, wu, wd, layer, rows_per_mod):
    t = x.shape[0]
    cap = 2 * t // N_EXPERTS
    nblk = t // TOKEN_BLOCK
    nj = cap // SLOT_TILE
    h, aff_t = moe_router(x, g2, mod, w_router_t, rows_per_mod)
    key, bnd = moe_route(aff_t, cap)
    cs_end = bnd[:, :nblk]
    tile_first = jnp.arange(nj, dtype=i32) * SLOT_TILE
    lo = jnp.sum(cs_end[:, None, :] <= tile_first[None, :, None], axis=-1, dtype=i32)
    hi = jnp.sum(cs_end[:, None, :] < (tile_first + SLOT_TILE)[None, :, None], axis=-1, dtype=i32)
    ye = moe_ffn(lo.reshape(-1), hi.reshape(-1), key, aff_t, h, wg, wu, wd, layer, cap)
    before = jnp.concatenate([jnp.zeros((N_EXPERTS, 1), i32), cs_end[:, :-1]], axis=1)
    starts = before & ~7
    nround = jnp.max((cs_end - starts + COMBINE_WINDOW - 1) // COMBINE_WINDOW, axis=0).astype(i32)
    return moe_combine(starts.T.reshape(-1), nround, key, ye, x, mod, cap, rows_per_mod)


def _final_norm_kernel(x_ref, g_ref, o_ref):
    o_ref[...] = _rms(x_ref[...], g_ref[...])


def final_norm(x, g, tm=1024):
    t = x.shape[0]
    return pl.pallas_call(
        _final_norm_kernel, grid=(t // tm,),
        in_specs=[pl.BlockSpec((tm, D), lambda i: (i, 0)), pl.BlockSpec((1, D), lambda i: (0, 0))],
        out_specs=pl.BlockSpec((tm, D), lambda i: (i, 0)),
        out_shape=jax.ShapeDtypeStruct((t, D), f32),
        compiler_params=_cparams("arbitrary"), name="final_norm",
    )(x, g)


def _rope_tables(seq):
    rows = seq // GRID_W
    row = jnp.repeat(jnp.arange(rows), GRID_W).astype(f32)
    col = jnp.tile(jnp.arange(GRID_W), rows).astype(f32)
    pairs = QK_ROPE // 4
    inv = ROPE_THETA ** (-jnp.arange(pairs, dtype=f32) / pairs)
    ang = jnp.concatenate([row[:, None] * inv, col[:, None] * inv], axis=-1)
    cos, sin = jnp.cos(ang), jnp.sin(ang)
    zero = jnp.zeros_like(cos)
    c = jnp.concatenate([cos, cos, zero, zero], axis=-1)
    s1 = jnp.concatenate([-sin, zero, zero, zero], axis=-1)
    s2 = jnp.concatenate([zero, sin, zero, zero], axis=-1)
    return c, s1, s2


def _mla_weights(w_dq, q_norm_g, w_uq, w_dkv, kv_norm_g, w_ukv, w_o):
    per_head = QK_NOPE + QK_ROPE
    uq = w_uq.reshape(Q_LORA, MLA_HEADS, per_head)
    uq_nope = uq[..., :QK_NOPE].reshape(Q_LORA, MLA_HEADS * QK_NOPE)
    uq_rope = jnp.pad(uq[..., QK_NOPE:], ((0, 0), (0, 0), (0, HEAD_PAD - QK_ROPE))).reshape(Q_LORA, MLA_HEADS * HEAD_PAD)
    wuq = jnp.concatenate([uq_nope, uq_rope], axis=1).astype(bf16)
    wdkv = jnp.pad(w_dkv, ((0, 0), (0, HEAD_PAD - QK_ROPE))).astype(bf16)
    ukv = w_ukv.reshape(KV_LORA, MLA_HEADS, QK_NOPE + V_HEAD)
    wukv = jnp.concatenate([ukv[..., :QK_NOPE].reshape(KV_LORA, -1), ukv[..., QK_NOPE:].reshape(KV_LORA, -1)], axis=1).astype(bf16)
    proj = (w_dq.astype(bf16), q_norm_g.reshape(1, Q_LORA), wuq, wdkv, kv_norm_g.reshape(1, KV_LORA), wukv)
    return proj, wukv, w_o.astype(bf16)


def _mla_layer(xp, xs, modp, mods, g1, cache_ckv_j, cache_krope_j, w, rope_tabs):
    proj_w, wukv, w_o = w
    bp, bs = xp.shape[0] // 256, xs.shape[0] // 2048
    nh = MLA_HEADS * HEAD_PAD
    qn, qr, kn, v, kr, ckv, krope = mla_project(xp, g1, modp, proj_w, None, xp.shape[0])
    r3 = lambda a, b: a.reshape(b, -1, a.shape[-1])
    op = mla_attention(r3(qn, bp), r3(qr, bp), r3(kn, bp), r3(kr, bp), r3(v, bp)).reshape(-1, nh)
    xp = matmul_residual(op, w_o, xp, modp, GATE1, xp.shape[0])
    new_ckv = ckv.reshape(bp, -1, KV_LORA)
    new_krope = krope[:, :QK_ROPE].reshape(bp, -1, QK_ROPE)
    qn, qr, kn, v, kr, _, _ = mla_project(xs, g1, mods, proj_w, rope_tabs, 2048)
    ctx = matmul_bf16(cache_ckv_j.reshape(-1, KV_LORA).astype(bf16), wukv, tm=512)
    nk = MLA_HEADS * QK_NOPE
    kn_all = jnp.concatenate([ctx[:, :nk].reshape(bs, -1, nk), r3(kn, bs)], axis=1)
    v_all = jnp.concatenate([ctx[:, nk:].reshape(bs, -1, nk), r3(v, bs)], axis=1)
    kr_ctx = jnp.pad(cache_krope_j, ((0, 0), (0, 0), (0, HEAD_PAD - QK_ROPE))).astype(bf16)
    kr_all = jnp.concatenate([kr_ctx, r3(kr, bs)], axis=1)
    os_ = mla_attention(r3(qn, bs), r3(qr, bs), kn_all, kr_all, v_all).reshape(-1, nh)
    xs = matmul_residual(os_, w_o, xs, mods, GATE1, 2048)
    return xp, xs, new_ckv, new_krope


def _ssm_stream(x, mod, g1, w, rows_per_mod, seq, h0):
    w_all, conv_w_all, bias_all, a_log, d_skip_row, norm_g, w_out = w
    b = x.shape[0] // seq
    z, xbc, dt = ssm_in_proj(x, g1, mod, w_all, conv_w_all, bias_all, rows_per_mod, seq)
    dt4 = dt[:, :2 * SSM_H].reshape(b, seq, 2, SSM_H)
    dt_dir = dt4.transpose(2, 0, 1, 3)
    dtT_dir = dt4.transpose(2, 0, 3, 1)
    xbc3 = xbc.reshape(b, seq, CONV_DIM)
    bT = xbc3[:, :, D_INNER:D_INNER + SSM_G * SSM_N].transpose(0, 2, 1)
    y, hout = ssd_scan(xbc3, z.reshape(b, seq, D_INNER), dt_dir, dtT_dir, bT, a_log, d_skip_row, norm_g, h0)
    x = matmul_residual(y.reshape(-1, D_INNER), w_out, x, mod, GATE1, rows_per_mod)
    new_state = hout.reshape(b, 2, SSM_N, SSM_H, SSM_P).transpose(0, 1, 3, 4, 2)
    return x, new_state


def kernel(x_prompt, x_sample, cache_ckv, cache_krope, state_ssm, c, c_ctx, w_mod, b_mod, norm1_g, norm2_g, final_norm_g, mla_w_dq, mla_q_norm_g, mla_w_uq, mla_w_dkv, mla_kv_norm_g, mla_w_ukv, mla_w_o, ssm_w_in, ssm_conv_w, ssm_conv_b, ssm_dt_bias, ssm_a_log, ssm_d_skip, ssm_norm_g, ssm_w_out, moe_w_router, moe_w_gate, moe_w_up, moe_w_down):
    bp, lp, _ = x_prompt.shape
    bs, ls, _ = x_sample.shape
    xp = x_prompt.reshape(bp * lp, D)
    xs = x_sample.reshape(bs * ls, D)

    cond8 = jnp.concatenate([c_ctx[None, :], c, jnp.zeros((8 - 1 - bs, D), f32)], axis=0)
    mod_all = modulation_all(cond8, w_mod, b_mod)
    rope_tabs = _rope_tables(ls)

    new_ckv, new_krope, new_ssm = [], [], []
    for l in range(DEPTH):
        modp = mod_all[l, 0:1].reshape(1, N_MOD, D)
        mods = mod_all[l, 1:1 + bs].reshape(bs, N_MOD, D)
        g1 = norm1_g[l].reshape(1, D)
        j = l // 2
        if l % 2 == 0:
            w = _mla_weights(mla_w_dq[j], mla_q_norm_g[j], mla_w_uq[j], mla_w_dkv[j], mla_kv_norm_g[j], mla_w_ukv[j], mla_w_o[j])
            xp, xs, ckv, krope = _mla_layer(xp, xs, modp, mods, g1, cache_ckv[:, j], cache_krope[:, j], w, rope_tabs)
            new_ckv.append(ckv)
            new_krope.append(krope)
        else:
            dt_cols = INPROJ_BLOCK - 2 * SSM_H
            w_all = jnp.pad(ssm_w_in[j], ((0, 0), (0, dt_cols))).astype(bf16)
            conv_w_all = jnp.pad(ssm_conv_w[j].T, ((0, 0), (D_INNER, INPROJ_BLOCK)))
            bias_all = jnp.concatenate([jnp.zeros((D_INNER,), f32), ssm_conv_b[j], ssm_dt_bias[j].reshape(-1),
                                        jnp.zeros((dt_cols,), f32)]).reshape(1, -1)
            w = (w_all, conv_w_all, bias_all, ssm_a_log[j],
                 jnp.repeat(ssm_d_skip[j], SSM_P).reshape(1, D_INNER), ssm_norm_g[j].reshape(1, D_INNER),
                 ssm_w_out[j].astype(bf16))
            xp, st_p = _ssm_stream(xp, modp, g1, w, bp * lp, lp, None)
            h0 = state_ssm[:, j].transpose(0, 1, 4, 2, 3).reshape(bs, 2, SSM_N, D_INNER)
            xs, _ = _ssm_stream(xs, mods, g1, w, ls, ls, h0)
            new_ssm.append(st_p)
        g2 = norm2_g[l].reshape(1, D)
        moe_w = (moe_w_router[l].T, moe_w_gate, moe_w_up, moe_w_down, l)
        xp = moe_layer(xp, g2, modp, *moe_w, bp * lp)
        xs = moe_layer(xs, g2, mods, *moe_w, ls)

    fg = final_norm_g.reshape(1, D)
    y_prompt = final_norm(xp, fg).reshape(bp, lp, D)
    y_sample = final_norm(xs, fg).reshape(bs, ls, D)
    return (y_prompt, y_sample, jnp.stack(new_ckv, axis=1), jnp.stack(new_krope, axis=1), jnp.stack(new_ssm, axis=1))
```

```python
import functools

import jax
import jax.numpy as jnp
from jax import lax
from jax.experimental import pallas as pl
from jax.experimental.pallas import tpu as pltpu

f32 = jnp.float32
bf16 = jnp.bfloat16
i32 = jnp.int32

D = 1024
DEPTH = 4
N_MOD = 6
EPS = 1e-6
GRID_W = 64

MLA_HEADS = 8
QK_NOPE = 128
QK_ROPE = 64
V_HEAD = 128
Q_LORA = 384
KV_LORA = 256
ROPE_THETA = 10000.0
ATTN_SCALE = (QK_NOPE + QK_ROPE) ** -0.5
HEAD_PAD = 128

D_INNER = 2 * D
SSM_P = 64
SSM_H = D_INNER // SSM_P
SSM_G = 4
SSM_N = 128
CHUNK = 128
CONV_DIM = D_INNER + 2 * SSM_G * SSM_N
HEADS_PER_GROUP = SSM_H // SSM_G
GROUP_W = HEADS_PER_GROUP * SSM_P

N_EXPERTS = 16
D_FF = 1024
TOKEN_BLOCK = 256
SLOT_TILE = 512
GATHER_TILE = 128
COMBINE_WINDOW = 64

VMEM_LIMIT = 56 * 1024 * 1024

SHIFT1, SCALE1, GATE1, SHIFT2, SCALE2, GATE2 = range(6)


def _cparams(*sem):
    return pltpu.CompilerParams(dimension_semantics=sem, vmem_limit_bytes=VMEM_LIMIT)


def _dot(a, b):
    return jnp.dot(a, b, preferred_element_type=f32)


def _dot_nt(a, b):
    return lax.dot_general(a, b, (((1,), (1,)), ((), ())), preferred_element_type=f32)


def _dot_tn(a, b):
    return lax.dot_general(a, b, (((0,), (0,)), ((), ())), preferred_element_type=f32)


def _dot_f32(a, b):
    return jnp.dot(a, b, preferred_element_type=f32, precision=lax.Precision.HIGHEST)


def _rms(x, g):
    ms = jnp.mean(x * x, axis=-1, keepdims=True)
    return x * lax.rsqrt(ms + EPS) * g


def _norm_mod(x, g, mod, k_shift, k_scale):
    return _rms(x, g) * (1.0 + mod[k_scale:k_scale + 1, :]) + mod[k_shift:k_shift + 1, :]


def _silu(x):
    return x * jax.nn.sigmoid(x)


def _mod_kernel(c_ref, w_ref, b_ref, o_ref):
    s = _silu(c_ref[...]).astype(bf16)
    o_ref[0] = _dot(s, w_ref[0].astype(bf16)) + b_ref[0]


def modulation_all(cond8, w_mod, b_mod):
    nb = 1536
    n = N_MOD * D
    return pl.pallas_call(
        _mod_kernel,
        grid=(DEPTH, n // nb),
        in_specs=[
            pl.BlockSpec((8, D), lambda l, j: (0, 0)),
            pl.BlockSpec((1, D, nb), lambda l, j: (l, 0, j)),
            pl.BlockSpec((1, 1, nb), lambda l, j: (l, 0, j)),
        ],
        out_specs=pl.BlockSpec((1, 8, nb), lambda l, j: (l, 0, j)),
        out_shape=jax.ShapeDtypeStruct((DEPTH, 8, n), f32),
        compiler_params=_cparams("arbitrary", "arbitrary"),
        name="modulation",
    )(cond8, w_mod, b_mod.reshape(DEPTH, 1, n))


def _rope_rot(p, c, s1, s2):
    return p * c + pltpu.roll(p, 96, axis=1) * s1 + pltpu.roll(p, 32, axis=1) * s2


def _mla_proj_kernel(*refs, rope):
    if rope:
        (x_ref, g_ref, mod_ref, wdq_ref, qg_ref, wuq_ref, wdkv_ref, kvg_ref, wukv_ref, rc_ref, rs1_ref, rs2_ref,
         qn_ref, qr_ref, kn_ref, v_ref, kr_ref, ckv_ref, krope_ref) = refs
    else:
        (x_ref, g_ref, mod_ref, wdq_ref, qg_ref, wuq_ref, wdkv_ref, kvg_ref, wukv_ref,
         qn_ref, qr_ref, kn_ref, v_ref, kr_ref, ckv_ref, krope_ref) = refs
    h = _norm_mod(x_ref[...], g_ref[...], mod_ref[0], SHIFT1, SCALE1).astype(bf16)
    nq = MLA_HEADS * QK_NOPE
    q_lat = _rms(_dot(h, wdq_ref[...]), qg_ref[...]).astype(bf16)
    q = _dot(q_lat, wuq_ref[...])
    qn_ref[...] = q[:, :nq].astype(bf16)
    if rope:
        c, s1, s2 = rc_ref[...], rs1_ref[...], rs2_ref[...]
    for hh in range(MLA_HEADS):
        piece = q[:, nq + hh * HEAD_PAD: nq + (hh + 1) * HEAD_PAD]
        if rope:
            piece = _rope_rot(piece, c, s1, s2)
        qr_ref[:, hh * HEAD_PAD:(hh + 1) * HEAD_PAD] = piece.astype(bf16)
    kv = _dot(h, wdkv_ref[...])
    ckv = _rms(kv[:, :KV_LORA], kvg_ref[...])
    ckv_ref[...] = ckv
    kr = kv[:, KV_LORA:]
    krope_ref[...] = kr
    if rope:
        kr = _rope_rot(kr, c, s1, s2)
    kr_ref[...] = kr.astype(bf16)
    kvx = _dot(ckv.astype(bf16), wukv_ref[...])
    nk = MLA_HEADS * QK_NOPE
    kn_ref[...] = kvx[:, :nk].astype(bf16)
    v_ref[...] = kvx[:, nk:].astype(bf16)


def mla_project(x, g, mod, w, rope_tabs, rows_per_mod, tm=256):
    t = x.shape[0]
    wdq, qg, wuq, wdkv, kvg, wukv = w
    full = lambda a: pl.BlockSpec(a.shape, lambda i: (0,) * a.ndim)
    row = lambda n: pl.BlockSpec((tm, n), lambda i: (i, 0))
    in_specs = [row(D), full(g), pl.BlockSpec((1, N_MOD, D), lambda i: ((i * tm) // rows_per_mod, 0, 0)),
                full(wdq), full(qg), full(wuq), full(wdkv), full(kvg), full(wukv)]
    args = [x, g, mod, wdq, qg, wuq, wdkv, kvg, wukv]
    rope = rope_tabs is not None
    if rope:
        nrb = rope_tabs[0].shape[0] // tm
        in_specs += [pl.BlockSpec((tm, HEAD_PAD), lambda i: (i % nrb, 0))] * 3
        args += list(rope_tabs)
    nh = MLA_HEADS * HEAD_PAD
    out_shape = [jax.ShapeDtypeStruct((t, nh), bf16)] * 4 + [
        jax.ShapeDtypeStruct((t, HEAD_PAD), bf16), jax.ShapeDtypeStruct((t, KV_LORA), f32),
        jax.ShapeDtypeStruct((t, HEAD_PAD), f32)]
    out_specs = [row(nh)] * 4 + [row(HEAD_PAD), row(KV_LORA), row(HEAD_PAD)]
    return pl.pallas_call(
        functools.partial(_mla_proj_kernel, rope=rope),
        grid=(t // tm,), in_specs=in_specs, out_specs=out_specs, out_shape=out_shape,
        compiler_params=_cparams("arbitrary"), name="mla_project",
    )(*args)


def _mm_kernel(a_ref, w_ref, o_ref):
    o_ref[...] = _dot(a_ref[...], w_ref[...]).astype(o_ref.dtype)


def matmul_bf16(a, w, tm):
    m, k = a.shape
    n = w.shape[1]
    return pl.pallas_call(
        _mm_kernel, grid=(m // tm,),
        in_specs=[pl.BlockSpec((tm, k), lambda i: (i, 0)), pl.BlockSpec((k, n), lambda i: (0, 0))],
        out_specs=pl.BlockSpec((tm, n), lambda i: (i, 0)),
        out_shape=jax.ShapeDtypeStruct((m, n), bf16),
        compiler_params=_cparams("arbitrary"), name="matmul_bf16",
    )(a, w)


def _attn_kernel(qn_ref, qr_ref, kn_ref, kr_ref, v_ref, o_ref):
    kr = kr_ref[0]
    for h in range(MLA_HEADS):
        sl = slice(h * HEAD_PAD, (h + 1) * HEAD_PAD)
        s = (_dot_nt(qn_ref[0, :, sl], kn_ref[0, :, sl]) + _dot_nt(qr_ref[0, :, sl], kr)) * ATTN_SCALE
        e = jnp.exp(s - jnp.max(s, axis=-1, keepdims=True))
        l = jnp.sum(e, axis=-1, keepdims=True)
        o = _dot(e.astype(bf16), v_ref[0, :, sl]) / l
        o_ref[0, :, sl] = o.astype(bf16)


def mla_attention(qn, qr, kn, kr, v, tq=256):
    b, lq, nh = qn.shape
    s = kn.shape[1]
    qspec = pl.BlockSpec((1, tq, nh), lambda i, j: (i, j, 0))
    kspec = pl.BlockSpec((1, s, nh), lambda i, j: (i, 0, 0))
    return pl.pallas_call(
        _attn_kernel, grid=(b, lq // tq),
        in_specs=[qspec, qspec, kspec, pl.BlockSpec((1, s, HEAD_PAD), lambda i, j: (i, 0, 0)), kspec],
        out_specs=qspec, out_shape=jax.ShapeDtypeStruct((b, lq, nh), bf16),
        compiler_params=_cparams("arbitrary", "arbitrary"), name="mla_attention",
    )(qn, qr, kn, kr, v)


def _mm_res_kernel(a_ref, w_ref, x_ref, mod_ref, o_ref, *, kgate):
    y = _dot(a_ref[...], w_ref[...])
    o_ref[...] = x_ref[...] + mod_ref[0][kgate:kgate + 1, :] * y


def matmul_residual(a, w, x, mod, kgate, rows_per_mod, tm=512):
    t, k = a.shape
    return pl.pallas_call(
        functools.partial(_mm_res_kernel, kgate=kgate), grid=(t // tm,),
        in_specs=[pl.BlockSpec((tm, k), lambda i: (i, 0)), pl.BlockSpec((k, D), lambda i: (0, 0)),
                  pl.BlockSpec((tm, D), lambda i: (i, 0)),
                  pl.BlockSpec((1, N_MOD, D), lambda i: ((i * tm) // rows_per_mod, 0, 0))],
        out_specs=pl.BlockSpec((tm, D), lambda i: (i, 0)),
        out_shape=jax.ShapeDtypeStruct((t, D), f32),
        compiler_params=_cparams("arbitrary"), name="matmul_residual",
    )(a, w, x, mod)


INPROJ_BLOCK = 512
INPROJ_Z_BLOCKS = D_INNER // INPROJ_BLOCK
INPROJ_CONV_BLOCKS = CONV_DIM // INPROJ_BLOCK
DT_PAD = 128


def _inproj_kernel(x_ref, g_ref, mod_ref, w_ref, cw_ref, b_ref, z_ref, xbc_ref, dt_ref, h_ref, *, period):
    j = pl.program_id(1)

    @pl.when(j == 0)
    def _():
        h_ref[...] = _norm_mod(x_ref[...], g_ref[...], mod_ref[0], SHIFT1, SCALE1).astype(bf16)

    @pl.when(j < INPROJ_Z_BLOCKS)
    def _():
        z_ref[...] = _dot(h_ref[...], w_ref[...])

    @pl.when((j >= INPROJ_Z_BLOCKS) & (j < INPROJ_Z_BLOCKS + INPROJ_CONV_BLOCKS))
    def _():
        y = _dot(h_ref[...], w_ref[...])
        tm = y.shape[0]
        pos = lax.broadcasted_iota(i32, (tm, 1), 0) & (period - 1)
        prev = jnp.where(pos == 0, 0.0, pltpu.roll(y, 1, axis=0))
        nxt = jnp.where(pos == period - 1, 0.0, pltpu.roll(y, tm - 1, axis=0))
        cw = cw_ref[...]
        xbc_ref[...] = _silu(cw[0:1, :] * prev + cw[1:2, :] * y + cw[2:3, :] * nxt + b_ref[...])

    @pl.when(j == INPROJ_Z_BLOCKS + INPROJ_CONV_BLOCKS)
    def _():
        y = _dot(h_ref[...], w_ref[:, :DT_PAD]) + b_ref[:, :DT_PAD]
        dt_ref[...] = jnp.maximum(y, 0.0) + jnp.log1p(jnp.exp(-jnp.abs(y)))


def ssm_in_proj(x, g, mod, w_all, conv_w_all, bias_all, rows_per_mod, period, tm=2048):
    t = x.shape[0]
    nb = INPROJ_BLOCK
    nz, nx = INPROJ_Z_BLOCKS, INPROJ_CONV_BLOCKS
    col = lambda rows: pl.BlockSpec((rows, nb), lambda i, j: (0, j))
    return pl.pallas_call(
        functools.partial(_inproj_kernel, period=period), grid=(t // tm, nz + nx + 1),
        in_specs=[pl.BlockSpec((tm, D), lambda i, j: (i, 0)), pl.BlockSpec((1, D), lambda i, j: (0, 0)),
                  pl.BlockSpec((1, N_MOD, D), lambda i, j: ((i * tm) // rows_per_mod, 0, 0)),
                  col(D), col(3), col(1)],
        out_specs=[pl.BlockSpec((tm, nb), lambda i, j: (i, jnp.minimum(j, nz - 1))),
                   pl.BlockSpec((tm, nb), lambda i, j: (i, jnp.clip(j - nz, 0, nx - 1))),
                   pl.BlockSpec((tm, DT_PAD), lambda i, j: (i, 0))],
        out_shape=[jax.ShapeDtypeStruct((t, D_INNER), f32), jax.ShapeDtypeStruct((t, CONV_DIM), f32),
                   jax.ShapeDtypeStruct((t, DT_PAD), f32)],
        scratch_shapes=[pltpu.VMEM((tm, D), bf16)],
        compiler_params=_cparams("arbitrary", "arbitrary"), name="ssm_in_proj",
    )(x, g, mod, w_all, conv_w_all, bias_all)


def _ssd_kernel(*refs, nc, has_h0):
    if has_h0:
        (xbc_ref, z_ref, dt_ref, dtT_ref, bT_ref, alr_ref, alc_ref, dsk_ref, ng_ref, h0_ref,
         y_ref, hout_ref, state_ref, yf_ref, yc_ref, col_ref, row_ref) = refs
    else:
        (xbc_ref, z_ref, dt_ref, dtT_ref, bT_ref, alr_ref, alc_ref, dsk_ref, ng_ref,
         y_ref, hout_ref, state_ref, yf_ref, yc_ref, col_ref, row_ref) = refs
    q = CHUNK
    d = pl.program_id(1)
    c = pl.program_id(2)
    ce = jnp.where(d == 0, c, nc - 1 - c)

    @pl.when(c == 0)
    def _():
        if has_h0:
            state_ref[...] = h0_ref[0, 0]
        else:
            state_ref[...] = jnp.zeros_like(state_ref)

    ii = lax.broadcasted_iota(i32, (q, q), 0)
    jj = lax.broadcasted_iota(i32, (q, q), 1)
    ahead = (ii - jj) * jnp.where(d == 0, 1, -1)
    causal = ahead >= 0
    first_head = lax.broadcasted_iota(i32, (1, 2 * SSM_P), 1) < SSM_P
    m_col = jnp.where(causal, 1.0, 0.0).astype(f32)
    m_row = jnp.where(ahead <= 0, 1.0, 0.0).astype(f32)

    a_row = -jnp.exp(alr_ref[0])
    a_col = -jnp.exp(alc_ref[0])
    dt = dt_ref[0, 0]
    dtT = dtT_ref[0, 0]
    daT = dtT * a_col
    cs_col = _dot_f32(m_col, dt * a_row)
    cs_row = _dot_f32(daT, m_row)
    tot = jnp.sum(daT, axis=1, keepdims=True)
    col_ref[0] = cs_col
    col_ref[1] = jnp.exp(cs_col)
    row_ref[0] = cs_row
    row_ref[1] = dtT
    row_ref[2] = jnp.exp(tot - cs_row) * dtT
    row_ref[3] = jnp.broadcast_to(jnp.exp(tot), (SSM_H, q))

    for g in range(SSM_G):
        b_g = xbc_ref[0, :, D_INNER + g * SSM_N: D_INNER + (g + 1) * SSM_N].astype(bf16)
        c_off = D_INNER + SSM_G * SSM_N
        c_g = xbc_ref[0, :, c_off + g * SSM_N: c_off + (g + 1) * SSM_N].astype(bf16)
        cb = _dot_nt(c_g, b_g)
        bT_g = bT_ref[0, g * SSM_N:(g + 1) * SSM_N, :]
        for pr in range(HEADS_PER_GROUP // 2):
            h0 = g * HEADS_PER_GROUP + 2 * pr
            h1 = h0 + 1
            ps = slice(h0 * SSM_P, (h1 + 1) * SSM_P)
            x2 = xbc_ref[0, :, ps]
            x_blk = jnp.concatenate([jnp.where(first_head, x2, 0.0), jnp.where(first_head, 0.0, x2)], axis=0).astype(bf16)
            st = state_ref[:, ps]
            ws, bs = [], []
            for h in (h0, h1):
                seg = col_ref[0, :, h:h + 1] - row_ref[0, h:h + 1, :]
                w = cb * jnp.exp(jnp.where(causal, seg, -jnp.inf)) * row_ref[1, h:h + 1, :]
                ws.append(w.astype(bf16))
                bs.append((bT_g * row_ref[2, h:h + 1, :]).astype(bf16))
            e_cs = jnp.where(first_head, col_ref[1, :, h0:h0 + 1], col_ref[1, :, h1:h1 + 1])
            yc_ref[:, ps] = _dot(jnp.concatenate(ws, axis=1), x_blk) + _dot(c_g, st.astype(bf16)) * e_cs
            dec = jnp.where(first_head, row_ref[3, h0:h0 + 1, :], row_ref[3, h1:h1 + 1, :])
            state_ref[:, ps] = st * dec + _dot(jnp.concatenate(bs, axis=1), x_blk)

    rows = pl.ds(pl.multiple_of(ce * q, q), q)

    @pl.when(d == 0)
    def _():
        yf_ref[rows, :] = yc_ref[...]

    @pl.when(d == 1)
    def _():
        y = yf_ref[rows, :] + yc_ref[...] + dsk_ref[...] * xbc_ref[0, :, :D_INNER]
        y = y * _silu(z_ref[0])
        y_ref[0] = _rms(y, ng_ref[...]).astype(bf16)

    @pl.when(c == nc - 1)
    def _():
        hout_ref[0, 0] = state_ref[...]


def ssd_scan(xbc, z, dt_dir, dtT_dir, bT, a_log, d_skip_row, norm_g, h0):
    b, l, _ = xbc.shape
    nc = l // CHUNK
    q = CHUNK
    has_h0 = h0 is not None
    ce = lambda d, c: jnp.where(d == 0, c, nc - 1 - c)
    late = lambda d, c: jnp.where(d == 0, nc - 1, nc - 1 - c)
    in_specs = [
        pl.BlockSpec((1, q, CONV_DIM), lambda i, d, c: (i, ce(d, c), 0)),
        pl.BlockSpec((1, q, D_INNER), lambda i, d, c: (i, late(d, c), 0)),
        pl.BlockSpec((1, 1, q, SSM_H), lambda i, d, c: (d, i, ce(d, c), 0)),
        pl.BlockSpec((1, 1, SSM_H, q), lambda i, d, c: (d, i, 0, ce(d, c))),
        pl.BlockSpec((1, SSM_G * SSM_N, q), lambda i, d, c: (i, 0, ce(d, c))),
        pl.BlockSpec((1, 1, SSM_H), lambda i, d, c: (d, 0, 0)),
        pl.BlockSpec((1, SSM_H, 1), lambda i, d, c: (d, 0, 0)),
        pl.BlockSpec((1, D_INNER), lambda i, d, c: (0, 0)),
        pl.BlockSpec((1, D_INNER), lambda i, d, c: (0, 0)),
    ]
    args = [xbc, z, dt_dir, dtT_dir, bT, a_log.reshape(2, 1, SSM_H), a_log.reshape(2, SSM_H, 1), d_skip_row, norm_g]
    st_spec = pl.BlockSpec((1, 1, SSM_N, D_INNER), lambda i, d, c: (i, d, 0, 0))
    if has_h0:
        in_specs.append(st_spec)
        args.append(h0)
    return pl.pallas_call(
        functools.partial(_ssd_kernel, nc=nc, has_h0=has_h0), grid=(b, 2, nc),
        in_specs=in_specs,
        out_specs=[pl.BlockSpec((1, q, D_INNER), lambda i, d, c: (i, late(d, c), 0)), st_spec],
        out_shape=[jax.ShapeDtypeStruct((b, l, D_INNER), bf16), jax.ShapeDtypeStruct((b, 2, SSM_N, D_INNER), f32)],
        scratch_shapes=[
            pltpu.VMEM((SSM_N, D_INNER), f32),
            pltpu.VMEM((l, D_INNER), f32),
            pltpu.VMEM((q, D_INNER), f32),
            pltpu.VMEM((2, q, SSM_H), f32),
            pltpu.VMEM((4, SSM_H, q), f32),
        ],
        compiler_params=_cparams("arbitrary", "arbitrary", "arbitrary"), name="ssd_scan",
    )(*args)


def _router_kernel(x_ref, g_ref, mod_ref, wr_ref, h_ref, aff_ref):
    h = _norm_mod(x_ref[...], g_ref[...], mod_ref[0], SHIFT2, SCALE2)
    hb = h.astype(bf16)
    h_ref[...] = hb
    h_lo = (h - hb.astype(f32)).astype(bf16)
    w = wr_ref[...]
    w_hi = w.astype(bf16)
    w_lo = (w - w_hi.astype(f32)).astype(bf16)
    lg = _dot_nt(w_hi, hb) + _dot_nt(w_hi, h_lo) + _dot_nt(w_lo, hb)
    e = jnp.exp(lg - jnp.max(lg, axis=0, keepdims=True))
    aff_ref[...] = e / jnp.sum(e, axis=0, keepdims=True)


def moe_router(x, g, mod, w_router_t, rows_per_mod, tm=512):
    t = x.shape[0]
    return pl.pallas_call(
        _router_kernel, grid=(t // tm,),
        in_specs=[pl.BlockSpec((tm, D), lambda i: (i, 0)), pl.BlockSpec((1, D), lambda i: (0, 0)),
                  pl.BlockSpec((1, N_MOD, D), lambda i: ((i * tm) // rows_per_mod, 0, 0)),
                  pl.BlockSpec((N_EXPERTS, D), lambda i: (0, 0))],
        out_specs=[pl.BlockSpec((tm, D), lambda i: (i, 0)), pl.BlockSpec((N_EXPERTS, tm), lambda i: (0, i))],
        out_shape=[jax.ShapeDtypeStruct((t, D), bf16), jax.ShapeDtypeStruct((N_EXPERTS, t), f32)],
        compiler_params=_cparams("arbitrary"), name="moe_router",
    )(x, g, mod, w_router_t)


THRESHOLD_REFINE_STEPS = 28
META_HI = 8
META_NROUND = 64


def _route_kernel(aff_ref, key_ref, start_ref, meta_ref, *, cap):
    t = aff_ref.shape[1]
    aff = aff_ref[...]

    def count_ge(v):
        return jnp.sum(jnp.where(aff >= v, 1.0, 0.0), axis=1, keepdims=True)

    bits = lax.bitcast_convert_type(aff, i32)
    tb = jnp.zeros((N_EXPERTS, 1), i32)
    for bit in range(30, -1, -1):
        cand = tb | (1 << bit)
        cnt = jnp.sum((bits >= cand).astype(i32), axis=1, keepdims=True)
        tb = jnp.where(cnt >= cap, cand, tb)
    approx = lax.bitcast_convert_type(tb, f32)
    lo = jnp.where(count_ge(0.5 * approx) >= cap, 0.5 * approx, 0.0)
    hi = jnp.where(count_ge(2.0 * approx) < cap, 2.0 * approx, 2.0)

    def refine(_, lh):
        lo, hi = lh
        mid = 0.5 * (lo + hi)
        ge = count_ge(mid) >= cap
        return jnp.where(ge, mid, lo), jnp.where(ge, hi, mid)

    thr, _ = lax.fori_loop(0, THRESHOLD_REFINE_STEPS, refine, (lo, hi))
    gt = aff > thr
    eq = aff == thr
    gtf = jnp.where(gt, 1.0, 0.0)
    eqf = jnp.where(eq, 1.0, 0.0)
    need = cap - jnp.sum(gtf, axis=1, keepdims=True).astype(i32)
    blk = TOKEN_BLOCK
    tri = (lax.broadcasted_iota(i32, (blk, blk), 0) <= lax.broadcasted_iota(i32, (blk, blk), 1))
    tri = jnp.where(tri, 1.0, 0.0).astype(bf16)
    carry = jnp.zeros((2 * N_EXPERTS, 1), f32)
    start_ref[...] = jnp.zeros_like(start_ref)
    meta_ref[...] = jnp.zeros_like(meta_ref)
    before = jnp.zeros((N_EXPERTS, 1), i32)
    nsub = cap // GATHER_TILE
    assert nsub <= META_HI
    first_blk = [jnp.zeros((N_EXPERTS, 1), i32) for _ in range(nsub)]
    last_blk = [jnp.zeros((N_EXPERTS, 1), i32) for _ in range(nsub)]
    for j in range(t // blk):
        sl = slice(j * blk, (j + 1) * blk)
        m = jnp.concatenate([gtf[:, sl], eqf[:, sl]], axis=0)
        pc = _dot(m.astype(bf16), tri) + carry
        carry = pc[:, blk - 1:blk]
        cs_gt = pc[:N_EXPERTS].astype(i32)
        cs_eq = pc[N_EXPERTS:].astype(i32)
        sel = gt[:, sl] | (eq[:, sl] & (cs_eq <= need))
        cs = cs_gt + jnp.minimum(cs_eq, need)
        key_ref[:, sl] = jnp.where(sel, cs, 0)
        end = cs[:, blk - 1:blk]
        start = before & ~7
        start_ref[:, j:j + 1] = start
        rounds = (end - start + (COMBINE_WINDOW - 1)) // COMBINE_WINDOW
        meta_ref[:, META_NROUND + j:META_NROUND + j + 1] = jnp.broadcast_to(
            jnp.max(rounds, axis=0, keepdims=True), (N_EXPERTS, 1))
        for s in range(nsub):
            first_blk[s] = first_blk[s] + jnp.where(end <= s * GATHER_TILE, 1, 0)
            last_blk[s] = last_blk[s] + jnp.where(end < (s + 1) * GATHER_TILE, 1, 0)
        before = end
    for s in range(nsub):
        meta_ref[:, s:s + 1] = first_blk[s]
        meta_ref[:, META_HI + s:META_HI + s + 1] = last_blk[s]


def moe_route(aff_t, cap):
    t = aff_t.shape[1]
    small = jax.ShapeDtypeStruct((N_EXPERTS, 128), i32)
    small_spec = pl.BlockSpec((N_EXPERTS, 128), lambda i: (0, 0))
    return pl.pallas_call(
        functools.partial(_route_kernel, cap=cap), grid=(1,),
        in_specs=[pl.BlockSpec((N_EXPERTS, t), lambda i: (0, 0))],
        out_specs=[pl.BlockSpec((N_EXPERTS, t), lambda i: (0, 0)), small_spec, small_spec],
        out_shape=[jax.ShapeDtypeStruct((N_EXPERTS, t), i32), small, small],
        compiler_params=_cparams("arbitrary"), name="moe_route",
    )(aff_t)


def _moe_ffn_kernel(meta_ref, key_ref, aff_ref, h_ref, wg_ref, wu_ref, wd_ref, o_ref, acc_ref, gate_ref):
    e = pl.program_id(0)
    j = pl.program_id(1)
    tm = acc_ref.shape[0]
    tb = TOKEN_BLOCK
    acc_ref[...] = jnp.zeros_like(acc_ref)
    gate_ref[...] = jnp.zeros_like(gate_ref)
    gt = GATHER_TILE
    for s in range(tm // gt):
        sub = j * (tm // gt) + s
        srows = slice(s * gt, (s + 1) * gt)
        slot = sub * gt + lax.broadcasted_iota(i32, (gt, 1), 0) + 1

        def body(b, carry, slot=slot, srows=srows):
            oh = key_ref[0, b] == slot
            rows = pl.ds(pl.multiple_of(b * tb, tb), tb)
            acc_ref[srows, :] += _dot(jnp.where(oh, 1.0, 0.0).astype(bf16), h_ref[rows, :])
            gate_ref[srows, :] += jnp.sum(jnp.where(oh, aff_ref[0, b], 0.0), axis=1, keepdims=True)
            return carry

        lax.fori_loop(meta_ref[e, sub], meta_ref[e, META_HI + sub] + 1, body, 0)
    xe = acc_ref[...].astype(bf16)
    hid = _silu(_dot(xe, wg_ref[0, 0].astype(bf16))) * _dot(xe, wu_ref[0, 0].astype(bf16))
    o_ref[0] = _dot(hid.astype(bf16), wd_ref[0, 0].astype(bf16)) * gate_ref[...]


def moe_ffn(meta, key, aff_t, h, wg, wu, wd, layer, cap):
    t = h.shape[0]
    tm = SLOT_TILE
    nblk = t // TOKEN_BLOCK
    nj = cap // tm
    key4 = key.reshape(N_EXPERTS, nblk, 1, TOKEN_BLOCK)
    aff4 = aff_t.reshape(N_EXPERTS, nblk, 1, TOKEN_BLOCK)
    row_spec = pl.BlockSpec((1, nblk, 1, TOKEN_BLOCK), lambda e, j, *_: (e, 0, 0, 0))
    wspec = pl.BlockSpec((1, 1, D, D_FF), lambda e, j, *_: (layer, e, 0, 0))
    grid_spec = pltpu.PrefetchScalarGridSpec(
        num_scalar_prefetch=1, grid=(N_EXPERTS, nj),
        in_specs=[row_spec, row_spec,
                  pl.BlockSpec((t, D), lambda e, j, *_: (0, 0), pipeline_mode=pl.Buffered(1)),
                  wspec, wspec, pl.BlockSpec((1, 1, D_FF, D), lambda e, j, *_: (layer, e, 0, 0))],
        out_specs=pl.BlockSpec((1, tm, D), lambda e, j, *_: (e, j, 0)),
        scratch_shapes=[pltpu.VMEM((tm, D), f32), pltpu.VMEM((tm, 1), f32)])
    return pl.pallas_call(
        _moe_ffn_kernel, grid_spec=grid_spec,
        out_shape=jax.ShapeDtypeStruct((N_EXPERTS, cap, D), f32),
        compiler_params=_cparams("arbitrary", "arbitrary"), name="moe_ffn",
    )(meta, key4, aff4, h, wg, wu, wd)


def _moe_combine_kernel(start_ref, meta_ref, key_ref, ye_hbm, x_ref, mod_ref, o_ref, stage_ref, acc_ref, sem,
                        *, cap, nblk):
    i = pl.program_id(0)
    w = COMBINE_WINDOW

    def window_start(tile, e, r):
        return start_ref[e, tile] + r * w

    def copies(tile, r, buf):
        out = []
        for e in range(N_EXPERTS):
            first = pl.multiple_of(jnp.minimum(window_start(tile, e, r), cap - w), 8)
            out.append(pltpu.make_async_copy(ye_hbm.at[e, pl.ds(first, w), :],
                                             stage_ref.at[buf, pl.ds(e * w, w), :], sem.at[buf]))
        return out

    def accumulate(r, buf):
        pieces = []
        for e in range(N_EXPERTS):
            begin = window_start(i, e, r)
            slot = jnp.minimum(begin, cap - w) + lax.broadcasted_iota(i32, (w, 1), 0) + 1
            oh = (key_ref[e:e + 1, :] == slot) & (slot > begin)
            pieces.append(jnp.where(oh, 1.0, 0.0))
        oh = jnp.concatenate(pieces, axis=0).astype(bf16)
        rows = stage_ref[buf]
        hi = rows.astype(bf16)
        lo = (rows - hi.astype(f32)).astype(bf16)
        acc_ref[...] += _dot_tn(oh, hi) + _dot_tn(oh, lo)

    cur = lax.rem(i, 2)

    @pl.when(i == 0)
    def _():
        for cp in copies(0, 0, 0):
            cp.start()

    @pl.when(i + 1 < nblk)
    def _():
        for cp in copies(i + 1, 0, 1 - cur):
            cp.start()

    acc_ref[...] = jnp.zeros_like(acc_ref)
    for cp in copies(i, 0, cur):
        cp.wait()
    accumulate(0, cur)

    def extra_round(r, carry):
        cps = copies(i, r, 2)
        for cp in cps:
            cp.start()
        for cp in cps:
            cp.wait()
        accumulate(r, 2)
        return carry

    lax.fori_loop(1, meta_ref[0, META_NROUND + i], extra_round, 0)
    o_ref[...] = x_ref[...] + mod_ref[0][GATE2:GATE2 + 1, :] * acc_ref[...]


def moe_combine(starts, meta, key, ye, x, mod, cap, rows_per_mod):
    t = x.shape[0]
    tb = TOKEN_BLOCK
    nblk = t // tb
    grid_spec = pltpu.PrefetchScalarGridSpec(
        num_scalar_prefetch=2, grid=(nblk,),
        in_specs=[pl.BlockSpec((N_EXPERTS, tb), lambda i, *_: (0, i)),
                  pl.BlockSpec(memory_space=pl.ANY),
                  pl.BlockSpec((tb, D), lambda i, *_: (i, 0)),
                  pl.BlockSpec((1, N_MOD, D), lambda i, *_: ((i * tb) // rows_per_mod, 0, 0))],
        out_specs=pl.BlockSpec((tb, D), lambda i, *_: (i, 0)),
        scratch_shapes=[pltpu.VMEM((3, N_EXPERTS * COMBINE_WINDOW, D), f32), pltpu.VMEM((tb, D), f32),
                        pltpu.SemaphoreType.DMA((3,))])
    return pl.pallas_call(
        functools.partial(_moe_combine_kernel, cap=cap, nblk=nblk), grid_spec=grid_spec,
        out_shape=jax.ShapeDtypeStruct((t, D), f32),
        compiler_params=_cparams("arbitrary"), name="moe_combine",
    )(starts, meta, key, ye, x, mod)


def moe_layer(x, g2, mod, w_router_t, wg, wu, wd, layer, rows_per_mod):
    t = x.shape[0]
    cap = 2 * t // N_EXPERTS
    h, aff_t = moe_router(x, g2, mod, w_router_t, rows_per_mod)
    key, starts, meta = moe_route(aff_t, cap)
    ye = moe_ffn(meta, key, aff_t, h, wg, wu, wd, layer, cap)
    return moe_combine(starts, meta, key, ye, x, mod, cap, rows_per_mod)


def _final_norm_kernel(x_ref, g_ref, o_ref):
    o_ref[...] = _rms(x_ref[...], g_ref[...])


def final_norm(x, g, tm=1024):
    t = x.shape[0]
    return pl.pallas_call(
        _final_norm_kernel, grid=(t // tm,),
        in_specs=[pl.BlockSpec((tm, D), lambda i: (i, 0)), pl.BlockSpec((1, D), lambda i: (0, 0))],
        out_specs=pl.BlockSpec((tm, D), lambda i: (i, 0)),
        out_shape=jax.ShapeDtypeStruct((t, D), f32),
        compiler_params=_cparams("arbitrary"), name="final_norm",
    )(x, g)


def _rope_tables(seq):
    rows = seq // GRID_W
    row = jnp.repeat(jnp.arange(rows), GRID_W).astype(f32)
    col = jnp.tile(jnp.arange(GRID_W), rows).astype(f32)
    pairs = QK_ROPE // 4
    inv = ROPE_THETA ** (-jnp.arange(pairs, dtype=f32) / pairs)
    ang = jnp.concatenate([row[:, None] * inv, col[:, None] * inv], axis=-1)
    cos, sin = jnp.cos(ang), jnp.sin(ang)
    zero = jnp.zeros_like(cos)
    c = jnp.concatenate([cos, cos, zero, zero], axis=-1)
    s1 = jnp.concatenate([-sin, zero, zero, zero], axis=-1)
    s2 = jnp.concatenate([zero, sin, zero, zero], axis=-1)
    return c, s1, s2


def _mla_weights(w_dq, q_norm_g, w_uq, w_dkv, kv_norm_g, w_ukv, w_o):
    per_head = QK_NOPE + QK_ROPE
    uq = w_uq.reshape(Q_LORA, MLA_HEADS, per_head)
    uq_nope = uq[..., :QK_NOPE].reshape(Q_LORA, MLA_HEADS * QK_NOPE)
    uq_rope = jnp.pad(uq[..., QK_NOPE:], ((0, 0), (0, 0), (0, HEAD_PAD - QK_ROPE))).reshape(Q_LORA, MLA_HEADS * HEAD_PAD)
    wuq = jnp.concatenate([uq_nope, uq_rope], axis=1).astype(bf16)
    wdkv = jnp.pad(w_dkv, ((0, 0), (0, HEAD_PAD - QK_ROPE))).astype(bf16)
    ukv = w_ukv.reshape(KV_LORA, MLA_HEADS, QK_NOPE + V_HEAD)
    wukv = jnp.concatenate([ukv[..., :QK_NOPE].reshape(KV_LORA, -1), ukv[..., QK_NOPE:].reshape(KV_LORA, -1)], axis=1).astype(bf16)
    proj = (w_dq.astype(bf16), q_norm_g.reshape(1, Q_LORA), wuq, wdkv, kv_norm_g.reshape(1, KV_LORA), wukv)
    return proj, wukv, w_o.astype(bf16)


def _mla_layer(xp, xs, modp, mods, g1, cache_ckv_j, cache_krope_j, w, rope_tabs):
    proj_w, wukv, w_o = w
    bp, bs = xp.shape[0] // 256, xs.shape[0] // 2048
    nh = MLA_HEADS * HEAD_PAD
    qn, qr, kn, v, kr, ckv, krope = mla_project(xp, g1, modp, proj_w, None, xp.shape[0])
    r3 = lambda a, b: a.reshape(b, -1, a.shape[-1])
    op = mla_attention(r3(qn, bp), r3(qr, bp), r3(kn, bp), r3(kr, bp), r3(v, bp)).reshape(-1, nh)
    xp = matmul_residual(op, w_o, xp, modp, GATE1, xp.shape[0])
    new_ckv = ckv.reshape(bp, -1, KV_LORA)
    new_krope = krope[:, :QK_ROPE].reshape(bp, -1, QK_ROPE)
    qn, qr, kn, v, kr, _, _ = mla_project(xs, g1, mods, proj_w, rope_tabs, 2048)
    ctx = matmul_bf16(cache_ckv_j.reshape(-1, KV_LORA).astype(bf16), wukv, tm=512)
    nk = MLA_HEADS * QK_NOPE
    kn_all = jnp.concatenate([ctx[:, :nk].reshape(bs, -1, nk), r3(kn, bs)], axis=1)
    v_all = jnp.concatenate([ctx[:, nk:].reshape(bs, -1, nk), r3(v, bs)], axis=1)
    kr_ctx = jnp.pad(cache_krope_j, ((0, 0), (0, 0), (0, HEAD_PAD - QK_ROPE))).astype(bf16)
    kr_all = jnp.concatenate([kr_ctx, r3(kr, bs)], axis=1)
    os_ = mla_attention(r3(qn, bs), r3(qr, bs), kn_all, kr_all, v_all).reshape(-1, nh)
    xs = matmul_residual(os_, w_o, xs, mods, GATE1, 2048)
    return xp, xs, new_ckv, new_krope


def _ssm_stream(x, mod, g1, w, rows_per_mod, seq, h0):
    w_all, conv_w_all, bias_all, a_log, d_skip_row, norm_g, w_out = w
    b = x.shape[0] // seq
    z, xbc, dt = ssm_in_proj(x, g1, mod, w_all, conv_w_all, bias_all, rows_per_mod, seq)
    dt4 = dt[:, :2 * SSM_H].reshape(b, seq, 2, SSM_H)
    dt_dir = dt4.transpose(2, 0, 1, 3)
    dtT_dir = dt4.transpose(2, 0, 3, 1)
    xbc3 = xbc.reshape(b, seq, CONV_DIM)
    bT = xbc3[:, :, D_INNER:D_INNER + SSM_G * SSM_N].transpose(0, 2, 1)
    y, hout = ssd_scan(xbc3, z.reshape(b, seq, D_INNER), dt_dir, dtT_dir, bT, a_log, d_skip_row, norm_g, h0)
    x = matmul_residual(y.reshape(-1, D_INNER), w_out, x, mod, GATE1, rows_per_mod)
    new_state = hout.reshape(b, 2, SSM_N, SSM_H, SSM_P).transpose(0, 1, 3, 4, 2)
    return x, new_state


def kernel(x_prompt, x_sample, cache_ckv, cache_krope, state_ssm, c, c_ctx, w_mod, b_mod, norm1_g, norm2_g, final_norm_g, mla_w_dq, mla_q_norm_g, mla_w_uq, mla_w_dkv, mla_kv_norm_g, mla_w_ukv, mla_w_o, ssm_w_in, ssm_conv_w, ssm_conv_b, ssm_dt_bias, ssm_a_log, ssm_d_skip, ssm_norm_g, ssm_w_out, moe_w_router, moe_w_gate, moe_w_up, moe_w_down):
    bp, lp, _ = x_prompt.shape
    bs, ls, _ = x_sample.shape
    xp = x_prompt.reshape(bp * lp, D)
    xs = x_sample.reshape(bs * ls, D)

    cond8 = jnp.concatenate([c_ctx[None, :], c, jnp.zeros((8 - 1 - bs, D), f32)], axis=0)
    mod_all = modulation_all(cond8, w_mod, b_mod)
    rope_tabs = _rope_tables(ls)

    new_ckv, new_krope, new_ssm = [], [], []
    for l in range(DEPTH):
        modp = mod_all[l, 0:1].reshape(1, N_MOD, D)
        mods = mod_all[l, 1:1 + bs].reshape(bs, N_MOD, D)
        g1 = norm1_g[l].reshape(1, D)
        j = l // 2
        if l % 2 == 0:
            w = _mla_weights(mla_w_dq[j], mla_q_norm_g[j], mla_w_uq[j], mla_w_dkv[j], mla_kv_norm_g[j], mla_w_ukv[j], mla_w_o[j])
            xp, xs, ckv, krope = _mla_layer(xp, xs, modp, mods, g1, cache_ckv[:, j], cache_krope[:, j], w, rope_tabs)
            new_ckv.append(ckv)
            new_krope.append(krope)
        else:
            dt_cols = INPROJ_BLOCK - 2 * SSM_H
            w_all = jnp.pad(ssm_w_in[j], ((0, 0), (0, dt_cols))).astype(bf16)
            conv_w_all = jnp.pad(ssm_conv_w[j].T, ((0, 0), (D_INNER, INPROJ_BLOCK)))
            bias_all = jnp.concatenate([jnp.zeros((D_INNER,), f32), ssm_conv_b[j], ssm_dt_bias[j].reshape(-1),
                                        jnp.zeros((dt_cols,), f32)]).reshape(1, -1)
            w = (w_all, conv_w_all, bias_all, ssm_a_log[j],
                 jnp.repeat(ssm_d_skip[j], SSM_P).reshape(1, D_INNER), ssm_norm_g[j].reshape(1, D_INNER),
                 ssm_w_out[j].astype(bf16))
            xp, st_p = _ssm_stream(xp, modp, g1, w, bp * lp, lp, None)
            h0 = state_ssm[:, j].transpose(0, 1, 4, 2, 3).reshape(bs, 2, SSM_N, D_INNER)
            xs, _ = _ssm_stream(xs, mods, g1, w, ls, ls, h0)
            new_ssm.append(st_p)
        g2 = norm2_g[l].reshape(1, D)
        moe_w = (moe_w_router[l].T, moe_w_gate, moe_w_up, moe_w_down, l)
        xp = moe_layer(xp, g2, modp, *moe_w, bp * lp)
        xs = moe_layer(xs, g2, mods, *moe_w, ls)

    fg = final_norm_g.reshape(1, D)
    y_prompt = final_norm(xp, fg).reshape(bp, lp, D)
    y_sample = final_norm(xs, fg).reshape(bs, ls, D)
    return (y_prompt, y_sample, jnp.stack(new_ckv, axis=1), jnp.stack(new_krope, axis=1), jnp.stack(new_ssm, axis=1))
```

```python
import functools

import jax
import jax.numpy as jnp
from jax import lax
from jax.experimental import pallas as pl
from jax.experimental.pallas import tpu as pltpu

f32 = jnp.float32
bf16 = jnp.bfloat16
i32 = jnp.int32

D = 1024
DEPTH = 4
N_MOD = 6
EPS = 1e-6
GRID_W = 64

MLA_HEADS = 8
QK_NOPE = 128
QK_ROPE = 64
V_HEAD = 128
Q_LORA = 384
KV_LORA = 256
ROPE_THETA = 10000.0
ATTN_SCALE = (QK_NOPE + QK_ROPE) ** -0.5
HEAD_PAD = 128

D_INNER = 2 * D
SSM_P = 64
SSM_H = D_INNER // SSM_P
SSM_G = 4
SSM_N = 128
CHUNK = 128
CONV_DIM = D_INNER + 2 * SSM_G * SSM_N
HEADS_PER_GROUP = SSM_H // SSM_G
GROUP_W = HEADS_PER_GROUP * SSM_P

N_EXPERTS = 16
D_FF = 1024
TOKEN_BLOCK = 256
SLOT_TILE = 512
GATHER_TILE = 128
COMBINE_WINDOW_PROMPT = 64
COMBINE_WINDOW_SAMPLE = 128

VMEM_LIMIT = 56 * 1024 * 1024

SHIFT1, SCALE1, GATE1, SHIFT2, SCALE2, GATE2 = range(6)


def _cparams(*sem):
    return pltpu.CompilerParams(dimension_semantics=sem, vmem_limit_bytes=VMEM_LIMIT)


def _dot(a, b):
    return jnp.dot(a, b, preferred_element_type=f32)


def _dot_nt(a, b):
    return lax.dot_general(a, b, (((1,), (1,)), ((), ())), preferred_element_type=f32)


def _dot_tn(a, b):
    return lax.dot_general(a, b, (((0,), (0,)), ((), ())), preferred_element_type=f32)


def _dot_f32(a, b):
    return jnp.dot(a, b, preferred_element_type=f32, precision=lax.Precision.HIGHEST)


def _rms(x, g):
    ms = jnp.mean(x * x, axis=-1, keepdims=True)
    return x * lax.rsqrt(ms + EPS) * g


def _norm_mod(x, g, mod, k_shift, k_scale):
    return _rms(x, g) * (1.0 + mod[k_scale:k_scale + 1, :]) + mod[k_shift:k_shift + 1, :]


def _silu(x):
    return x * jax.nn.sigmoid(x)


def _mod_kernel(c_ref, w_ref, b_ref, o_ref):
    s = _silu(c_ref[...]).astype(bf16)
    o_ref[0] = _dot(s, w_ref[0].astype(bf16)) + b_ref[0]


def modulation_all(cond8, w_mod, b_mod):
    nb = 1536
    n = N_MOD * D
    return pl.pallas_call(
        _mod_kernel,
        grid=(DEPTH, n // nb),
        in_specs=[
            pl.BlockSpec((8, D), lambda l, j: (0, 0)),
            pl.BlockSpec((1, D, nb), lambda l, j: (l, 0, j)),
            pl.BlockSpec((1, 1, nb), lambda l, j: (l, 0, j)),
        ],
        out_specs=pl.BlockSpec((1, 8, nb), lambda l, j: (l, 0, j)),
        out_shape=jax.ShapeDtypeStruct((DEPTH, 8, n), f32),
        compiler_params=_cparams("arbitrary", "arbitrary"),
        name="modulation",
    )(cond8, w_mod, b_mod.reshape(DEPTH, 1, n))


def _rope_rot(p, c, s1, s2):
    return p * c + pltpu.roll(p, 96, axis=1) * s1 + pltpu.roll(p, 32, axis=1) * s2


def _mla_proj_kernel(*refs, rope):
    if rope:
        (x_ref, g_ref, mod_ref, wdq_ref, qg_ref, wuq_ref, wdkv_ref, kvg_ref, wukv_ref, rc_ref, rs1_ref, rs2_ref,
         qn_ref, qr_ref, kn_ref, v_ref, kr_ref, ckv_ref, krope_ref) = refs
    else:
        (x_ref, g_ref, mod_ref, wdq_ref, qg_ref, wuq_ref, wdkv_ref, kvg_ref, wukv_ref,
         qn_ref, qr_ref, kn_ref, v_ref, kr_ref, ckv_ref, krope_ref) = refs
    h = _norm_mod(x_ref[...], g_ref[...], mod_ref[0], SHIFT1, SCALE1).astype(bf16)
    nq = MLA_HEADS * QK_NOPE
    q_lat = _rms(_dot(h, wdq_ref[...]), qg_ref[...]).astype(bf16)
    q = _dot(q_lat, wuq_ref[...])
    qn_ref[...] = q[:, :nq].astype(bf16)
    if rope:
        c, s1, s2 = rc_ref[...], rs1_ref[...], rs2_ref[...]
    for hh in range(MLA_HEADS):
        piece = q[:, nq + hh * HEAD_PAD: nq + (hh + 1) * HEAD_PAD]
        if rope:
            piece = _rope_rot(piece, c, s1, s2)
        qr_ref[:, hh * HEAD_PAD:(hh + 1) * HEAD_PAD] = piece.astype(bf16)
    kv = _dot(h, wdkv_ref[...])
    ckv = _rms(kv[:, :KV_LORA], kvg_ref[...])
    ckv_ref[...] = ckv
    kr = kv[:, KV_LORA:]
    krope_ref[...] = kr
    if rope:
        kr = _rope_rot(kr, c, s1, s2)
    kr_ref[...] = kr.astype(bf16)
    kvx = _dot(ckv.astype(bf16), wukv_ref[...])
    nk = MLA_HEADS * QK_NOPE
    kn_ref[...] = kvx[:, :nk].astype(bf16)
    v_ref[...] = kvx[:, nk:].astype(bf16)


def mla_project(x, g, mod, w, rope_tabs, rows_per_mod, tm=256):
    t = x.shape[0]
    wdq, qg, wuq, wdkv, kvg, wukv = w
    full = lambda a: pl.BlockSpec(a.shape, lambda i: (0,) * a.ndim)
    row = lambda n: pl.BlockSpec((tm, n), lambda i: (i, 0))
    in_specs = [row(D), full(g), pl.BlockSpec((1, N_MOD, D), lambda i: ((i * tm) // rows_per_mod, 0, 0)),
                full(wdq), full(qg), full(wuq), full(wdkv), full(kvg), full(wukv)]
    args = [x, g, mod, wdq, qg, wuq, wdkv, kvg, wukv]
    rope = rope_tabs is not None
    if rope:
        nrb = rope_tabs[0].shape[0] // tm
        in_specs += [pl.BlockSpec((tm, HEAD_PAD), lambda i: (i % nrb, 0))] * 3
        args += list(rope_tabs)
    nh = MLA_HEADS * HEAD_PAD
    out_shape = [jax.ShapeDtypeStruct((t, nh), bf16)] * 4 + [
        jax.ShapeDtypeStruct((t, HEAD_PAD), bf16), jax.ShapeDtypeStruct((t, KV_LORA), f32),
        jax.ShapeDtypeStruct((t, HEAD_PAD), f32)]
    out_specs = [row(nh)] * 4 + [row(HEAD_PAD), row(KV_LORA), row(HEAD_PAD)]
    return pl.pallas_call(
        functools.partial(_mla_proj_kernel, rope=rope),
        grid=(t // tm,), in_specs=in_specs, out_specs=out_specs, out_shape=out_shape,
        compiler_params=_cparams("arbitrary"), name="mla_project",
    )(*args)


def _mm_kernel(a_ref, w_ref, o_ref):
    o_ref[...] = _dot(a_ref[...], w_ref[...]).astype(o_ref.dtype)


def matmul_bf16(a, w, tm):
    m, k = a.shape
    n = w.shape[1]
    return pl.pallas_call(
        _mm_kernel, grid=(m // tm,),
        in_specs=[pl.BlockSpec((tm, k), lambda i: (i, 0)), pl.BlockSpec((k, n), lambda i: (0, 0))],
        out_specs=pl.BlockSpec((tm, n), lambda i: (i, 0)),
        out_shape=jax.ShapeDtypeStruct((m, n), bf16),
        compiler_params=_cparams("arbitrary"), name="matmul_bf16",
    )(a, w)


def _attn_kernel(qn_ref, qr_ref, kn_ref, kr_ref, v_ref, o_ref):
    kr = kr_ref[0]
    for h in range(MLA_HEADS):
        sl = slice(h * HEAD_PAD, (h + 1) * HEAD_PAD)
        s = (_dot_nt(qn_ref[0, :, sl], kn_ref[0, :, sl]) + _dot_nt(qr_ref[0, :, sl], kr)) * ATTN_SCALE
        e = jnp.exp(s - jnp.max(s, axis=-1, keepdims=True))
        l = jnp.sum(e, axis=-1, keepdims=True)
        o = _dot(e.astype(bf16), v_ref[0, :, sl]) / l
        o_ref[0, :, sl] = o.astype(bf16)


def mla_attention(qn, qr, kn, kr, v, tq=256):
    b, lq, nh = qn.shape
    s = kn.shape[1]
    qspec = pl.BlockSpec((1, tq, nh), lambda i, j: (i, j, 0))
    kspec = pl.BlockSpec((1, s, nh), lambda i, j: (i, 0, 0))
    return pl.pallas_call(
        _attn_kernel, grid=(b, lq // tq),
        in_specs=[qspec, qspec, kspec, pl.BlockSpec((1, s, HEAD_PAD), lambda i, j: (i, 0, 0)), kspec],
        out_specs=qspec, out_shape=jax.ShapeDtypeStruct((b, lq, nh), bf16),
        compiler_params=_cparams("arbitrary", "arbitrary"), name="mla_attention",
    )(qn, qr, kn, kr, v)


def _mm_res_kernel(a_ref, w_ref, x_ref, mod_ref, o_ref, *, kgate):
    y = _dot(a_ref[...], w_ref[...])
    o_ref[...] = x_ref[...] + mod_ref[0][kgate:kgate + 1, :] * y


def matmul_residual(a, w, x, mod, kgate, rows_per_mod, tm=512):
    t, k = a.shape
    return pl.pallas_call(
        functools.partial(_mm_res_kernel, kgate=kgate), grid=(t // tm,),
        in_specs=[pl.BlockSpec((tm, k), lambda i: (i, 0)), pl.BlockSpec((k, D), lambda i: (0, 0)),
                  pl.BlockSpec((tm, D), lambda i: (i, 0)),
                  pl.BlockSpec((1, N_MOD, D), lambda i: ((i * tm) // rows_per_mod, 0, 0))],
        out_specs=pl.BlockSpec((tm, D), lambda i: (i, 0)),
        out_shape=jax.ShapeDtypeStruct((t, D), f32),
        compiler_params=_cparams("arbitrary"), name="matmul_residual",
    )(a, w, x, mod)


INPROJ_BLOCK = 512
INPROJ_Z_BLOCKS = D_INNER // INPROJ_BLOCK
INPROJ_CONV_BLOCKS = CONV_DIM // INPROJ_BLOCK
DT_PAD = 128


def _inproj_kernel(x_ref, g_ref, mod_ref, w_ref, cw_ref, b_ref, z_ref, xbc_ref, dt_ref, h_ref, *, period):
    j = pl.program_id(1)

    @pl.when(j == 0)
    def _():
        h_ref[...] = _norm_mod(x_ref[...], g_ref[...], mod_ref[0], SHIFT1, SCALE1).astype(bf16)

    @pl.when(j < INPROJ_Z_BLOCKS)
    def _():
        z_ref[...] = _dot(h_ref[...], w_ref[...])

    @pl.when((j >= INPROJ_Z_BLOCKS) & (j < INPROJ_Z_BLOCKS + INPROJ_CONV_BLOCKS))
    def _():
        y = _dot(h_ref[...], w_ref[...])
        tm = y.shape[0]
        pos = lax.broadcasted_iota(i32, (tm, 1), 0) & (period - 1)
        prev = jnp.where(pos == 0, 0.0, pltpu.roll(y, 1, axis=0))
        nxt = jnp.where(pos == period - 1, 0.0, pltpu.roll(y, tm - 1, axis=0))
        cw = cw_ref[...]
        xbc_ref[...] = _silu(cw[0:1, :] * prev + cw[1:2, :] * y + cw[2:3, :] * nxt + b_ref[...])

    @pl.when(j == INPROJ_Z_BLOCKS + INPROJ_CONV_BLOCKS)
    def _():
        y = _dot(h_ref[...], w_ref[:, :DT_PAD]) + b_ref[:, :DT_PAD]
        dt_ref[...] = jnp.maximum(y, 0.0) + jnp.log1p(jnp.exp(-jnp.abs(y)))


def ssm_in_proj(x, g, mod, w_all, conv_w_all, bias_all, rows_per_mod, period, tm=2048):
    t = x.shape[0]
    nb = INPROJ_BLOCK
    nz, nx = INPROJ_Z_BLOCKS, INPROJ_CONV_BLOCKS
    col = lambda rows: pl.BlockSpec((rows, nb), lambda i, j: (0, j))
    return pl.pallas_call(
        functools.partial(_inproj_kernel, period=period), grid=(t // tm, nz + nx + 1),
        in_specs=[pl.BlockSpec((tm, D), lambda i, j: (i, 0)), pl.BlockSpec((1, D), lambda i, j: (0, 0)),
                  pl.BlockSpec((1, N_MOD, D), lambda i, j: ((i * tm) // rows_per_mod, 0, 0)),
                  col(D), col(3), col(1)],
        out_specs=[pl.BlockSpec((tm, nb), lambda i, j: (i, jnp.minimum(j, nz - 1))),
                   pl.BlockSpec((tm, nb), lambda i, j: (i, jnp.clip(j - nz, 0, nx - 1))),
                   pl.BlockSpec((tm, DT_PAD), lambda i, j: (i, 0))],
        out_shape=[jax.ShapeDtypeStruct((t, D_INNER), f32), jax.ShapeDtypeStruct((t, CONV_DIM), f32),
                   jax.ShapeDtypeStruct((t, DT_PAD), f32)],
        scratch_shapes=[pltpu.VMEM((tm, D), bf16)],
        compiler_params=_cparams("arbitrary", "arbitrary"), name="ssm_in_proj",
    )(x, g, mod, w_all, conv_w_all, bias_all)


def _ssd_kernel(*refs, nc, has_h0, has_hout, n_prev):
    xbc_ref, z_ref, dt_ref, dtT_ref, bT_ref, alr_ref, alc_ref, dsk_ref, ng_ref = refs[:9]
    rest = list(refs[9:])
    h0_ref = rest.pop(0) if has_h0 else None
    prev_ref = rest.pop(0) if n_prev else None
    y_ref = rest.pop(0)
    hout_ref = rest.pop(0) if has_hout else None
    state_ref, yf_ref, yc_ref, col_ref, row_ref = rest
    q = CHUNK
    d = pl.program_id(1)
    c = pl.program_id(2)
    ce = jnp.where(d == 0, c, nc - 1 - c)

    @pl.when(c == 0)
    def _():
        if has_h0:
            state_ref[...] = h0_ref[0, 0]
        else:
            state_ref[...] = jnp.zeros_like(state_ref)

    ii = lax.broadcasted_iota(i32, (q, q), 0)
    jj = lax.broadcasted_iota(i32, (q, q), 1)
    ahead = (ii - jj) * jnp.where(d == 0, 1, -1)
    causal = ahead >= 0
    first_head = lax.broadcasted_iota(i32, (1, 2 * SSM_P), 1) < SSM_P
    m_col = jnp.where(causal, 1.0, 0.0).astype(f32)
    m_row = jnp.where(ahead <= 0, 1.0, 0.0).astype(f32)

    a_row = -jnp.exp(alr_ref[0])
    a_col = -jnp.exp(alc_ref[0])
    dt = dt_ref[0, 0]
    dtT = dtT_ref[0, 0]
    daT = dtT * a_col
    cs_col = _dot_f32(m_col, dt * a_row)
    cs_row = _dot_f32(daT, m_row)
    tot = jnp.sum(daT, axis=1, keepdims=True)
    col_ref[0] = cs_col
    col_ref[1] = jnp.exp(cs_col)
    row_ref[0] = cs_row
    row_ref[1] = dtT
    row_ref[2] = jnp.exp(tot - cs_row) * dtT
    row_ref[3] = jnp.broadcast_to(jnp.exp(tot), (SSM_H, q))

    for g in range(SSM_G):
        b_g = xbc_ref[0, :, D_INNER + g * SSM_N: D_INNER + (g + 1) * SSM_N].astype(bf16)
        c_off = D_INNER + SSM_G * SSM_N
        c_g = xbc_ref[0, :, c_off + g * SSM_N: c_off + (g + 1) * SSM_N].astype(bf16)
        cb = _dot_nt(c_g, b_g)
        bT_g = bT_ref[0, g * SSM_N:(g + 1) * SSM_N, :]
        for pr in range(HEADS_PER_GROUP // 2):
            h0 = g * HEADS_PER_GROUP + 2 * pr
            h1 = h0 + 1
            ps = slice(h0 * SSM_P, (h1 + 1) * SSM_P)
            x2 = xbc_ref[0, :, ps]
            x_blk = jnp.concatenate([jnp.where(first_head, x2, 0.0), jnp.where(first_head, 0.0, x2)], axis=0).astype(bf16)
            st = state_ref[:, ps]
            ws, bs = [], []
            for h in (h0, h1):
                seg = col_ref[0, :, h:h + 1] - row_ref[0, h:h + 1, :]
                w = cb * jnp.exp(jnp.where(causal, seg, -jnp.inf)) * row_ref[1, h:h + 1, :]
                ws.append(w.astype(bf16))
                bs.append((bT_g * row_ref[2, h:h + 1, :]).astype(bf16))
            e_cs = jnp.where(first_head, col_ref[1, :, h0:h0 + 1], col_ref[1, :, h1:h1 + 1])
            yc_ref[:, ps] = _dot(jnp.concatenate(ws, axis=1), x_blk) + _dot(c_g, st.astype(bf16)) * e_cs
            dec = jnp.where(first_head, row_ref[3, h0:h0 + 1, :], row_ref[3, h1:h1 + 1, :])
            state_ref[:, ps] = st * dec + _dot(jnp.concatenate(bs, axis=1), x_blk)

    rows = pl.ds(pl.multiple_of(ce * q, q), q)

    @pl.when(d == 0)
    def _():
        yf_ref[rows, :] = yc_ref[...]

    @pl.when(d == 1)
    def _():
        y = yf_ref[rows, :] + yc_ref[...] + dsk_ref[...] * xbc_ref[0, :, :D_INNER]
        y = y * _silu(z_ref[0])
        y_ref[0] = _rms(y, ng_ref[...]).astype(bf16)

    if has_hout:
        @pl.when(c == nc - 1)
        def _():
            if n_prev:
                hout_ref[0, :n_prev, 0] = prev_ref[0, :, 0]
            hout_ref[0, n_prev, 0] = state_ref[...].T


def ssd_scan(xbc, z, dt_dir, dtT_dir, bT, a_log, d_skip_row, norm_g, h0, want_state=False, prev_states=None):
    b, l, _ = xbc.shape
    nc = l // CHUNK
    q = CHUNK
    has_h0 = h0 is not None
    has_hout = want_state
    n_prev = 0 if prev_states is None else prev_states.shape[1]
    ce = lambda d, c: jnp.where(d == 0, c, nc - 1 - c)
    late = lambda d, c: jnp.where(d == 0, nc - 1, nc - 1 - c)
    in_specs = [
        pl.BlockSpec((1, q, CONV_DIM), lambda i, d, c: (i, ce(d, c), 0)),
        pl.BlockSpec((1, q, D_INNER), lambda i, d, c: (i, late(d, c), 0)),
        pl.BlockSpec((1, 1, q, SSM_H), lambda i, d, c: (d, i, ce(d, c), 0)),
        pl.BlockSpec((1, 1, SSM_H, q), lambda i, d, c: (d, i, 0, ce(d, c))),
        pl.BlockSpec((1, SSM_G * SSM_N, q), lambda i, d, c: (i, 0, ce(d, c))),
        pl.BlockSpec((1, 1, SSM_H), lambda i, d, c: (d, 0, 0)),
        pl.BlockSpec((1, SSM_H, 1), lambda i, d, c: (d, 0, 0)),
        pl.BlockSpec((1, D_INNER), lambda i, d, c: (0, 0)),
        pl.BlockSpec((1, D_INNER), lambda i, d, c: (0, 0)),
    ]
    args = [xbc, z, dt_dir, dtT_dir, bT, a_log.reshape(2, 1, SSM_H), a_log.reshape(2, SSM_H, 1), d_skip_row, norm_g]
    if has_h0:
        in_specs.append(pl.BlockSpec((1, 1, SSM_N, D_INNER), lambda i, d, c: (i, d, 0, 0)))
        args.append(h0)
    out_specs = [pl.BlockSpec((1, q, D_INNER), lambda i, d, c: (i, late(d, c), 0))]
    out_shape = [jax.ShapeDtypeStruct((b, l, D_INNER), bf16)]
    if has_hout:
        if n_prev:
            in_specs.append(pl.BlockSpec((1, n_prev, 1, D_INNER, SSM_N), lambda i, d, c: (i, 0, d, 0, 0)))
            args.append(prev_states)
        out_specs.append(pl.BlockSpec((1, n_prev + 1, 1, D_INNER, SSM_N), lambda i, d, c: (i, 0, d, 0, 0)))
        out_shape.append(jax.ShapeDtypeStruct((b, n_prev + 1, 2, D_INNER, SSM_N), f32))
    return pl.pallas_call(
        functools.partial(_ssd_kernel, nc=nc, has_h0=has_h0, has_hout=has_hout, n_prev=n_prev),
        grid=(b, 2, nc),
        in_specs=in_specs, out_specs=out_specs, out_shape=out_shape,
        scratch_shapes=[
            pltpu.VMEM((SSM_N, D_INNER), f32),
            pltpu.VMEM((l, D_INNER), f32),
            pltpu.VMEM((q, D_INNER), f32),
            pltpu.VMEM((2, q, SSM_H), f32),
            pltpu.VMEM((4, SSM_H, q), f32),
        ],
        compiler_params=_cparams("arbitrary", "arbitrary", "arbitrary"), name="ssd_scan",
    )(*args)


def _router_kernel(x_ref, g_ref, mod_ref, wr_ref, h_ref, aff_ref):
    h = _norm_mod(x_ref[...], g_ref[...], mod_ref[0], SHIFT2, SCALE2)
    hb = h.astype(bf16)
    h_ref[...] = hb
    h_lo = (h - hb.astype(f32)).astype(bf16)
    w = wr_ref[...]
    w_hi = w.astype(bf16)
    w_lo = (w - w_hi.astype(f32)).astype(bf16)
    lg = _dot_nt(w_hi, hb) + _dot_nt(w_hi, h_lo) + _dot_nt(w_lo, hb)
    e = jnp.exp(lg - jnp.max(lg, axis=0, keepdims=True))
    aff_ref[...] = e / jnp.sum(e, axis=0, keepdims=True)


def moe_router(x, g, mod, w_router_t, rows_per_mod, tm=512):
    t = x.shape[0]
    return pl.pallas_call(
        _router_kernel, grid=(t // tm,),
        in_specs=[pl.BlockSpec((tm, D), lambda i: (i, 0)), pl.BlockSpec((1, D), lambda i: (0, 0)),
                  pl.BlockSpec((1, N_MOD, D), lambda i: ((i * tm) // rows_per_mod, 0, 0)),
                  pl.BlockSpec((N_EXPERTS, D), lambda i: (0, 0))],
        out_specs=[pl.BlockSpec((tm, D), lambda i: (i, 0)), pl.BlockSpec((N_EXPERTS, tm), lambda i: (0, i))],
        out_shape=[jax.ShapeDtypeStruct((t, D), bf16), jax.ShapeDtypeStruct((N_EXPERTS, t), f32)],
        compiler_params=_cparams("arbitrary"), name="moe_router",
    )(x, g, mod, w_router_t)


THRESHOLD_REFINE_STEPS = 28
META_HI = 8
META_NROUND = 64


def _route_kernel(aff_ref, key_ref, start_ref, meta_ref, *, cap, window):
    t = aff_ref.shape[1]
    aff = aff_ref[...]

    def count_ge(v):
        return jnp.sum(jnp.where(aff >= v, 1.0, 0.0), axis=1, keepdims=True)

    bits = lax.bitcast_convert_type(aff, i32)
    tb = jnp.zeros((N_EXPERTS, 1), i32)
    for bit in range(30, -1, -1):
        cand = tb | (1 << bit)
        cnt = jnp.sum((bits >= cand).astype(i32), axis=1, keepdims=True)
        tb = jnp.where(cnt >= cap, cand, tb)
    approx = lax.bitcast_convert_type(tb, f32)
    lo = jnp.where(count_ge(0.5 * approx) >= cap, 0.5 * approx, 0.0)
    hi = jnp.where(count_ge(2.0 * approx) < cap, 2.0 * approx, 2.0)

    def refine(_, lh):
        lo, hi = lh
        mid = 0.5 * (lo + hi)
        ge = count_ge(mid) >= cap
        return jnp.where(ge, mid, lo), jnp.where(ge, hi, mid)

    thr, _ = lax.fori_loop(0, THRESHOLD_REFINE_STEPS, refine, (lo, hi))
    gt = aff > thr
    eq = aff == thr
    gtf = jnp.where(gt, 1.0, 0.0)
    eqf = jnp.where(eq, 1.0, 0.0)
    need = cap - jnp.sum(gtf, axis=1, keepdims=True).astype(i32)
    blk = TOKEN_BLOCK
    tri = (lax.broadcasted_iota(i32, (blk, blk), 0) <= lax.broadcasted_iota(i32, (blk, blk), 1))
    tri = jnp.where(tri, 1.0, 0.0).astype(bf16)
    carry = jnp.zeros((2 * N_EXPERTS, 1), f32)
    start_ref[...] = jnp.zeros_like(start_ref)
    meta_ref[...] = jnp.zeros_like(meta_ref)
    before = jnp.zeros((N_EXPERTS, 1), i32)
    nsub = cap // GATHER_TILE
    assert nsub <= META_HI
    first_blk = [jnp.zeros((N_EXPERTS, 1), i32) for _ in range(nsub)]
    last_blk = [jnp.zeros((N_EXPERTS, 1), i32) for _ in range(nsub)]
    for j in range(t // blk):
        sl = slice(j * blk, (j + 1) * blk)
        m = jnp.concatenate([gtf[:, sl], eqf[:, sl]], axis=0)
        pc = _dot(m.astype(bf16), tri) + carry
        carry = pc[:, blk - 1:blk]
        cs_gt = pc[:N_EXPERTS].astype(i32)
        cs_eq = pc[N_EXPERTS:].astype(i32)
        sel = gt[:, sl] | (eq[:, sl] & (cs_eq <= need))
        cs = cs_gt + jnp.minimum(cs_eq, need)
        key_ref[:, sl] = jnp.where(sel, cs, 0)
        end = cs[:, blk - 1:blk]
        start = before & ~7
        start_ref[:, j:j + 1] = start
        rounds = (end - start + (window - 1)) // window
        meta_ref[:, META_NROUND + j:META_NROUND + j + 1] = jnp.broadcast_to(
            jnp.max(rounds, axis=0, keepdims=True), (N_EXPERTS, 1))
        for s in range(nsub):
            first_blk[s] = first_blk[s] + jnp.where(end <= s * GATHER_TILE, 1, 0)
            last_blk[s] = last_blk[s] + jnp.where(end < (s + 1) * GATHER_TILE, 1, 0)
        before = end
    for s in range(nsub):
        meta_ref[:, s:s + 1] = first_blk[s]
        meta_ref[:, META_HI + s:META_HI + s + 1] = last_blk[s]


def moe_route(aff_t, cap, window):
    t = aff_t.shape[1]
    small = jax.ShapeDtypeStruct((N_EXPERTS, 128), i32)
    small_spec = pl.BlockSpec((N_EXPERTS, 128), lambda i: (0, 0))
    return pl.pallas_call(
        functools.partial(_route_kernel, cap=cap, window=window), grid=(1,),
        in_specs=[pl.BlockSpec((N_EXPERTS, t), lambda i: (0, 0))],
        out_specs=[pl.BlockSpec((N_EXPERTS, t), lambda i: (0, 0)), small_spec, small_spec],
        out_shape=[jax.ShapeDtypeStruct((N_EXPERTS, t), i32), small, small],
        compiler_params=_cparams("arbitrary"), name="moe_route",
    )(aff_t)


def _moe_ffn_kernel(meta_ref, key_ref, aff_ref, h_ref, wg_ref, wu_ref, wd_ref, o_ref, acc_ref, gate_ref):
    e = pl.program_id(0)
    j = pl.program_id(1)
    tm = acc_ref.shape[0]
    tb = TOKEN_BLOCK
    acc_ref[...] = jnp.zeros_like(acc_ref)
    gate_ref[...] = jnp.zeros_like(gate_ref)
    gt = GATHER_TILE
    nsub = tm // gt
    nblk = key_ref.shape[1]
    first = [meta_ref[e, j * nsub + s] for s in range(nsub)]
    last = [meta_ref[e, META_HI + j * nsub + s] for s in range(nsub)]
    steps = functools.reduce(jnp.maximum, [last[s] - first[s] for s in range(nsub)]) + 1
    row_id = lax.broadcasted_iota(i32, (gt, 1), 0)

    def body(k, carry):
        for s in range(nsub):
            srows = slice(s * gt, (s + 1) * gt)
            b = first[s] + k
            slot = jnp.where(b <= last[s], (j * nsub + s) * gt + 1, -gt) + row_id
            b = jnp.minimum(b, nblk - 1)
            oh = key_ref[0, b] == slot
            rows = pl.ds(pl.multiple_of(b * tb, tb), tb)
            acc_ref[srows, :] += _dot(jnp.where(oh, 1.0, 0.0).astype(bf16), h_ref[rows, :])
            gate_ref[srows, :] += jnp.sum(jnp.where(oh, aff_ref[0, b], 0.0), axis=1, keepdims=True)
        return carry

    lax.fori_loop(0, steps, body, 0)
    xe = acc_ref[...].astype(bf16)
    hid = _silu(_dot(xe, wg_ref[0, 0].astype(bf16))) * _dot(xe, wu_ref[0, 0].astype(bf16))
    o_ref[0] = _dot(hid.astype(bf16), wd_ref[0, 0].astype(bf16)) * gate_ref[...]


def moe_ffn(meta, key, aff_t, h, wg, wu, wd, layer, cap):
    t = h.shape[0]
    tm = SLOT_TILE
    nblk = t // TOKEN_BLOCK
    nj = cap // tm
    key4 = key.reshape(N_EXPERTS, nblk, 1, TOKEN_BLOCK)
    aff4 = aff_t.reshape(N_EXPERTS, nblk, 1, TOKEN_BLOCK)
    row_spec = pl.BlockSpec((1, nblk, 1, TOKEN_BLOCK), lambda e, j, *_: (e, 0, 0, 0))
    wspec = pl.BlockSpec((1, 1, D, D_FF), lambda e, j, *_: (layer, e, 0, 0))
    grid_spec = pltpu.PrefetchScalarGridSpec(
        num_scalar_prefetch=1, grid=(N_EXPERTS, nj),
        in_specs=[row_spec, row_spec,
                  pl.BlockSpec((t, D), lambda e, j, *_: (0, 0), pipeline_mode=pl.Buffered(1)),
                  wspec, wspec, pl.BlockSpec((1, 1, D_FF, D), lambda e, j, *_: (layer, e, 0, 0))],
        out_specs=pl.BlockSpec((1, tm, D), lambda e, j, *_: (e, j, 0)),
        scratch_shapes=[pltpu.VMEM((tm, D), f32), pltpu.VMEM((tm, 1), f32)])
    return pl.pallas_call(
        _moe_ffn_kernel, grid_spec=grid_spec,
        out_shape=jax.ShapeDtypeStruct((N_EXPERTS, cap, D), f32),
        compiler_params=_cparams("arbitrary", "arbitrary"), name="moe_ffn",
    )(meta, key4, aff4, h, wg, wu, wd)


def _moe_combine_kernel(start_ref, meta_ref, key_ref, ye_hbm, x_ref, mod_ref, o_ref, stage_ref, acc_ref, sem,
                        *, cap, nblk, window):
    i = pl.program_id(0)
    w = window

    def window_start(tile, e, r):
        return start_ref[e, tile] + r * w

    def copies(tile, r, buf):
        out = []
        for e in range(N_EXPERTS):
            first = pl.multiple_of(jnp.minimum(window_start(tile, e, r), cap - w), 8)
            out.append(pltpu.make_async_copy(ye_hbm.at[e, pl.ds(first, w), :],
                                             stage_ref.at[buf, pl.ds(e * w, w), :], sem.at[buf]))
        return out

    def accumulate(r, buf):
        pieces = []
        for e in range(N_EXPERTS):
            begin = window_start(i, e, r)
            slot = jnp.minimum(begin, cap - w) + lax.broadcasted_iota(i32, (w, 1), 0) + 1
            oh = (key_ref[e:e + 1, :] == slot) & (slot > begin)
            pieces.append(jnp.where(oh, 1.0, 0.0))
        oh = jnp.concatenate(pieces, axis=0).astype(bf16)
        rows = stage_ref[buf]
        hi = rows.astype(bf16)
        lo = (rows - hi.astype(f32)).astype(bf16)
        acc_ref[...] += _dot_tn(oh, hi) + _dot_tn(oh, lo)

    cur = lax.rem(i, 2)

    @pl.when(i == 0)
    def _():
        for cp in copies(0, 0, 0):
            cp.start()

    @pl.when(i + 1 < nblk)
    def _():
        for cp in copies(i + 1, 0, 1 - cur):
            cp.start()

    acc_ref[...] = jnp.zeros_like(acc_ref)
    for cp in copies(i, 0, cur):
        cp.wait()
    accumulate(0, cur)

    def extra_round(r, carry):
        cps = copies(i, r, 2)
        for cp in cps:
            cp.start()
        for cp in cps:
            cp.wait()
        accumulate(r, 2)
        return carry

    lax.fori_loop(1, meta_ref[0, META_NROUND + i], extra_round, 0)
    o_ref[...] = x_ref[...] + mod_ref[0][GATE2:GATE2 + 1, :] * acc_ref[...]


def moe_combine(starts, meta, key, ye, x, mod, cap, rows_per_mod, window):
    t = x.shape[0]
    tb = TOKEN_BLOCK
    nblk = t // tb
    grid_spec = pltpu.PrefetchScalarGridSpec(
        num_scalar_prefetch=2, grid=(nblk,),
        in_specs=[pl.BlockSpec((N_EXPERTS, tb), lambda i, *_: (0, i)),
                  pl.BlockSpec(memory_space=pl.ANY),
                  pl.BlockSpec((tb, D), lambda i, *_: (i, 0)),
                  pl.BlockSpec((1, N_MOD, D), lambda i, *_: ((i * tb) // rows_per_mod, 0, 0))],
        out_specs=pl.BlockSpec((tb, D), lambda i, *_: (i, 0)),
        scratch_shapes=[pltpu.VMEM((3, N_EXPERTS * window, D), f32), pltpu.VMEM((tb, D), f32),
                        pltpu.SemaphoreType.DMA((3,))])
    return pl.pallas_call(
        functools.partial(_moe_combine_kernel, cap=cap, nblk=nblk, window=window), grid_spec=grid_spec,
        out_shape=jax.ShapeDtypeStruct((t, D), f32),
        compiler_params=_cparams("arbitrary"), name="moe_combine",
    )(starts, meta, key, ye, x, mod)


def moe_layer(x, g2, mod, w_router_t, wg, wu, wd, layer, rows_per_mod, window):
    t = x.shape[0]
    cap = 2 * t // N_EXPERTS
    h, aff_t = moe_router(x, g2, mod, w_router_t, rows_per_mod)
    key, starts, meta = moe_route(aff_t, cap, window)
    ye = moe_ffn(meta, key, aff_t, h, wg, wu, wd, layer, cap)
    return moe_combine(starts, meta, key, ye, x, mod, cap, rows_per_mod, window)


def _final_norm_kernel(x_ref, g_ref, o_ref):
    o_ref[...] = _rms(x_ref[...], g_ref[...])


def final_norm(x, g, tm=1024):
    t = x.shape[0]
    return pl.pallas_call(
        _final_norm_kernel, grid=(t // tm,),
        in_specs=[pl.BlockSpec((tm, D), lambda i: (i, 0)), pl.BlockSpec((1, D), lambda i: (0, 0))],
        out_specs=pl.BlockSpec((tm, D), lambda i: (i, 0)),
        out_shape=jax.ShapeDtypeStruct((t, D), f32),
        compiler_params=_cparams("arbitrary"), name="final_norm",
    )(x, g)


def _rope_tables(seq):
    rows = seq // GRID_W
    row = jnp.repeat(jnp.arange(rows), GRID_W).astype(f32)
    col = jnp.tile(jnp.arange(GRID_W), rows).astype(f32)
    pairs = QK_ROPE // 4
    inv = ROPE_THETA ** (-jnp.arange(pairs, dtype=f32) / pairs)
    ang = jnp.concatenate([row[:, None] * inv, col[:, None] * inv], axis=-1)
    cos, sin = jnp.cos(ang), jnp.sin(ang)
    zero = jnp.zeros_like(cos)
    c = jnp.concatenate([cos, cos, zero, zero], axis=-1)
    s1 = jnp.concatenate([-sin, zero, zero, zero], axis=-1)
    s2 = jnp.concatenate([zero, sin, zero, zero], axis=-1)
    return c, s1, s2


def _mla_weights(w_dq, q_norm_g, w_uq, w_dkv, kv_norm_g, w_ukv, w_o):
    per_head = QK_NOPE + QK_ROPE
    uq = w_uq.reshape(Q_LORA, MLA_HEADS, per_head)
    uq_nope = uq[..., :QK_NOPE].reshape(Q_LORA, MLA_HEADS * QK_NOPE)
    uq_rope = jnp.pad(uq[..., QK_NOPE:], ((0, 0), (0, 0), (0, HEAD_PAD - QK_ROPE))).reshape(Q_LORA, MLA_HEADS * HEAD_PAD)
    wuq = jnp.concatenate([uq_nope, uq_rope], axis=1).astype(bf16)
    wdkv = jnp.pad(w_dkv, ((0, 0), (0, HEAD_PAD - QK_ROPE))).astype(bf16)
    ukv = w_ukv.reshape(KV_LORA, MLA_HEADS, QK_NOPE + V_HEAD)
    wukv = jnp.concatenate([ukv[..., :QK_NOPE].reshape(KV_LORA, -1), ukv[..., QK_NOPE:].reshape(KV_LORA, -1)], axis=1).astype(bf16)
    proj = (w_dq.astype(bf16), q_norm_g.reshape(1, Q_LORA), wuq, wdkv, kv_norm_g.reshape(1, KV_LORA), wukv)
    return proj, wukv, w_o.astype(bf16)


def _mla_layer(xp, xs, modp, mods, g1, cache_ckv_j, cache_krope_j, w, rope_tabs):
    proj_w, wukv, w_o = w
    bp, bs = xp.shape[0] // 256, xs.shape[0] // 2048
    nh = MLA_HEADS * HEAD_PAD
    qn, qr, kn, v, kr, ckv, krope = mla_project(xp, g1, modp, proj_w, None, xp.shape[0])
    r3 = lambda a, b: a.reshape(b, -1, a.shape[-1])
    op = mla_attention(r3(qn, bp), r3(qr, bp), r3(kn, bp), r3(kr, bp), r3(v, bp)).reshape(-1, nh)
    xp = matmul_residual(op, w_o, xp, modp, GATE1, xp.shape[0])
    new_ckv = ckv.reshape(bp, -1, KV_LORA)
    new_krope = krope[:, :QK_ROPE].reshape(bp, -1, QK_ROPE)
    qn, qr, kn, v, kr, _, _ = mla_project(xs, g1, mods, proj_w, rope_tabs, 2048)
    ctx = matmul_bf16(cache_ckv_j.reshape(-1, KV_LORA).astype(bf16), wukv, tm=512)
    nk = MLA_HEADS * QK_NOPE
    kn_all = jnp.concatenate([ctx[:, :nk].reshape(bs, -1, nk), r3(kn, bs)], axis=1)
    v_all = jnp.concatenate([ctx[:, nk:].reshape(bs, -1, nk), r3(v, bs)], axis=1)
    kr_ctx = jnp.pad(cache_krope_j, ((0, 0), (0, 0), (0, HEAD_PAD - QK_ROPE))).astype(bf16)
    kr_all = jnp.concatenate([kr_ctx, r3(kr, bs)], axis=1)
    os_ = mla_attention(r3(qn, bs), r3(qr, bs), kn_all, kr_all, v_all).reshape(-1, nh)
    xs = matmul_residual(os_, w_o, xs, mods, GATE1, 2048)
    return xp, xs, new_ckv, new_krope


def _ssm_stream(x, mod, g1, w, rows_per_mod, seq, h0, want_state=False, prev_states=None):
    w_all, conv_w_all, bias_all, a_log, d_skip_row, norm_g, w_out = w
    b = x.shape[0] // seq
    z, xbc, dt = ssm_in_proj(x, g1, mod, w_all, conv_w_all, bias_all, rows_per_mod, seq)
    dt4 = dt[:, :2 * SSM_H].reshape(b, seq, 2, SSM_H)
    dt_dir = dt4.transpose(2, 0, 1, 3)
    dtT_dir = dt4.transpose(2, 0, 3, 1)
    xbc3 = xbc.reshape(b, seq, CONV_DIM)
    bT = xbc3[:, :, D_INNER:D_INNER + SSM_G * SSM_N].transpose(0, 2, 1)
    outs = ssd_scan(xbc3, z.reshape(b, seq, D_INNER), dt_dir, dtT_dir, bT, a_log, d_skip_row, norm_g, h0,
                    want_state, prev_states)
    x = matmul_residual(outs[0].reshape(-1, D_INNER), w_out, x, mod, GATE1, rows_per_mod)
    return x, (outs[1] if want_state else None)


def kernel(x_prompt, x_sample, cache_ckv, cache_krope, state_ssm, c, c_ctx, w_mod, b_mod, norm1_g, norm2_g, final_norm_g, mla_w_dq, mla_q_norm_g, mla_w_uq, mla_w_dkv, mla_kv_norm_g, mla_w_ukv, mla_w_o, ssm_w_in, ssm_conv_w, ssm_conv_b, ssm_dt_bias, ssm_a_log, ssm_d_skip, ssm_norm_g, ssm_w_out, moe_w_router, moe_w_gate, moe_w_up, moe_w_down):
    bp, lp, _ = x_prompt.shape
    bs, ls, _ = x_sample.shape
    xp = x_prompt.reshape(bp * lp, D)
    xs = x_sample.reshape(bs * ls, D)

    cond8 = jnp.concatenate([c_ctx[None, :], c, jnp.zeros((8 - 1 - bs, D), f32)], axis=0)
    mod_all = modulation_all(cond8, w_mod, b_mod)
    rope_tabs = _rope_tables(ls)

    new_ckv, new_krope = [], []
    new_ssm = None
    for l in range(DEPTH):
        modp = mod_all[l, 0:1].reshape(1, N_MOD, D)
        mods = mod_all[l, 1:1 + bs].reshape(bs, N_MOD, D)
        g1 = norm1_g[l].reshape(1, D)
        j = l // 2
        if l % 2 == 0:
            w = _mla_weights(mla_w_dq[j], mla_q_norm_g[j], mla_w_uq[j], mla_w_dkv[j], mla_kv_norm_g[j], mla_w_ukv[j], mla_w_o[j])
            xp, xs, ckv, krope = _mla_layer(xp, xs, modp, mods, g1, cache_ckv[:, j], cache_krope[:, j], w, rope_tabs)
            new_ckv.append(ckv)
            new_krope.append(krope)
        else:
            dt_cols = INPROJ_BLOCK - 2 * SSM_H
            w_all = jnp.pad(ssm_w_in[j], ((0, 0), (0, dt_cols))).astype(bf16)
            conv_w_all = jnp.pad(ssm_conv_w[j].T, ((0, 0), (D_INNER, INPROJ_BLOCK)))
            bias_all = jnp.concatenate([jnp.zeros((D_INNER,), f32), ssm_conv_b[j], ssm_dt_bias[j].reshape(-1),
                                        jnp.zeros((dt_cols,), f32)]).reshape(1, -1)
            w = (w_all, conv_w_all, bias_all, ssm_a_log[j],
                 jnp.repeat(ssm_d_skip[j], SSM_P).reshape(1, D_INNER), ssm_norm_g[j].reshape(1, D_INNER),
                 ssm_w_out[j].astype(bf16))
            xp, new_ssm = _ssm_stream(xp, modp, g1, w, bp * lp, lp, None, want_state=True, prev_states=new_ssm)
            h0 = state_ssm[:, j].transpose(0, 1, 4, 2, 3).reshape(bs, 2, SSM_N, D_INNER)
            xs, _ = _ssm_stream(xs, mods, g1, w, ls, ls, h0)
        g2 = norm2_g[l].reshape(1, D)
        moe_w = (moe_w_router[l].T, moe_w_gate, moe_w_up, moe_w_down, l)
        xp = moe_layer(xp, g2, modp, *moe_w, bp * lp, COMBINE_WINDOW_PROMPT)
        xs = moe_layer(xs, g2, mods, *moe_w, ls, COMBINE_WINDOW_SAMPLE)

    fg = final_norm_g.reshape(1, D)
    y_prompt = final_norm(xp, fg).reshape(bp, lp, D)
    y_sample = final_norm(xs, fg).reshape(bs, ls, D)
    new_ssm = new_ssm.reshape(bp, DEPTH // 2, 2, SSM_H, SSM_P, SSM_N)
    return (y_prompt, y_sample, jnp.stack(new_ckv, axis=1), jnp.stack(new_krope, axis=1), new_ssm)
```

```python
import functools

import jax
import jax.numpy as jnp
from jax import lax
from jax.experimental import pallas as pl
from jax.experimental.pallas import tpu as pltpu

f32 = jnp.float32
bf16 = jnp.bfloat16
i32 = jnp.int32

D = 1024
DEPTH = 4
N_MOD = 6
EPS = 1e-6
GRID_W = 64

MLA_HEADS = 8
QK_NOPE = 128
QK_ROPE = 64
V_HEAD = 128
Q_LORA = 384
KV_LORA = 256
ROPE_THETA = 10000.0
ATTN_SCALE = (QK_NOPE + QK_ROPE) ** -0.5
HEAD_PAD = 128

D_INNER = 2 * D
SSM_P = 64
SSM_H = D_INNER // SSM_P
SSM_G = 4
SSM_N = 128
CHUNK = 128
CONV_DIM = D_INNER + 2 * SSM_G * SSM_N
HEADS_PER_GROUP = SSM_H // SSM_G
GROUP_W = HEADS_PER_GROUP * SSM_P

N_EXPERTS = 16
D_FF = 1024
TOKEN_BLOCK = 256
SLOT_TILE = 512
GATHER_TILE = 128
GATHER_BLOCKS = 8
COMBINE_WINDOW_PROMPT = 64
COMBINE_WINDOW_SAMPLE = 128

VMEM_LIMIT = 56 * 1024 * 1024

SHIFT1, SCALE1, GATE1, SHIFT2, SCALE2, GATE2 = range(6)


def _cparams(*sem):
    return pltpu.CompilerParams(dimension_semantics=sem, vmem_limit_bytes=VMEM_LIMIT)


def _dot(a, b):
    return jnp.dot(a, b, preferred_element_type=f32)


def _dot_nt(a, b):
    return lax.dot_general(a, b, (((1,), (1,)), ((), ())), preferred_element_type=f32)


def _dot_tn(a, b):
    return lax.dot_general(a, b, (((0,), (0,)), ((), ())), preferred_element_type=f32)


def _dot_f32(a, b):
    return jnp.dot(a, b, preferred_element_type=f32, precision=lax.Precision.HIGHEST)


def _rms(x, g):
    ms = jnp.mean(x * x, axis=-1, keepdims=True)
    return x * lax.rsqrt(ms + EPS) * g


def _norm_mod(x, g, mod, k_shift, k_scale):
    return _rms(x, g) * (1.0 + mod[k_scale:k_scale + 1, :]) + mod[k_shift:k_shift + 1, :]


def _silu(x):
    return x * jax.nn.sigmoid(x)


def _mod_kernel(c_ref, w_ref, b_ref, o_ref):
    s = _silu(c_ref[...]).astype(bf16)
    o_ref[0] = _dot(s, w_ref[0].astype(bf16)) + b_ref[0]


def modulation_all(cond8, w_mod, b_mod):
    nb = 1536
    n = N_MOD * D
    return pl.pallas_call(
        _mod_kernel,
        grid=(DEPTH, n // nb),
        in_specs=[
            pl.BlockSpec((8, D), lambda l, j: (0, 0)),
            pl.BlockSpec((1, D, nb), lambda l, j: (l, 0, j)),
            pl.BlockSpec((1, 1, nb), lambda l, j: (l, 0, j)),
        ],
        out_specs=pl.BlockSpec((1, 8, nb), lambda l, j: (l, 0, j)),
        out_shape=jax.ShapeDtypeStruct((DEPTH, 8, n), f32),
        compiler_params=_cparams("arbitrary", "arbitrary"),
        name="modulation",
    )(cond8, w_mod, b_mod.reshape(DEPTH, 1, n))


def _rope_rot(p, c, s1, s2):
    return p * c + pltpu.roll(p, 96, axis=1) * s1 + pltpu.roll(p, 32, axis=1) * s2


def _mla_proj_kernel(*refs, rope):
    if rope:
        (x_ref, g_ref, mod_ref, wdq_ref, qg_ref, wuq_ref, wdkv_ref, kvg_ref, wukv_ref, rc_ref, rs1_ref, rs2_ref,
         qn_ref, qr_ref, kn_ref, v_ref, kr_ref, ckv_ref, krope_ref) = refs
    else:
        (x_ref, g_ref, mod_ref, wdq_ref, qg_ref, wuq_ref, wdkv_ref, kvg_ref, wukv_ref,
         qn_ref, qr_ref, kn_ref, v_ref, kr_ref, ckv_ref, krope_ref) = refs
    h = _norm_mod(x_ref[...], g_ref[...], mod_ref[0], SHIFT1, SCALE1).astype(bf16)
    nq = MLA_HEADS * QK_NOPE
    q_lat = _rms(_dot(h, wdq_ref[...]), qg_ref[...]).astype(bf16)
    q = _dot(q_lat, wuq_ref[...])
    qn_ref[...] = q[:, :nq].astype(bf16)
    if rope:
        c, s1, s2 = rc_ref[...], rs1_ref[...], rs2_ref[...]
    for hh in range(MLA_HEADS):
        piece = q[:, nq + hh * HEAD_PAD: nq + (hh + 1) * HEAD_PAD]
        if rope:
            piece = _rope_rot(piece, c, s1, s2)
        qr_ref[:, hh * HEAD_PAD:(hh + 1) * HEAD_PAD] = piece.astype(bf16)
    kv = _dot(h, wdkv_ref[...])
    ckv = _rms(kv[:, :KV_LORA], kvg_ref[...])
    ckv_ref[...] = ckv
    kr = kv[:, KV_LORA:]
    krope_ref[...] = kr
    if rope:
        kr = _rope_rot(kr, c, s1, s2)
    kr_ref[...] = kr.astype(bf16)
    kvx = _dot(ckv.astype(bf16), wukv_ref[...])
    nk = MLA_HEADS * QK_NOPE
    kn_ref[...] = kvx[:, :nk].astype(bf16)
    v_ref[...] = kvx[:, nk:].astype(bf16)


def mla_project(x, g, mod, w, rope_tabs, rows_per_mod, tm=256):
    t = x.shape[0]
    wdq, qg, wuq, wdkv, kvg, wukv = w
    full = lambda a: pl.BlockSpec(a.shape, lambda i: (0,) * a.ndim)
    row = lambda n: pl.BlockSpec((tm, n), lambda i: (i, 0))
    in_specs = [row(D), full(g), pl.BlockSpec((1, N_MOD, D), lambda i: ((i * tm) // rows_per_mod, 0, 0)),
                full(wdq), full(qg), full(wuq), full(wdkv), full(kvg), full(wukv)]
    args = [x, g, mod, wdq, qg, wuq, wdkv, kvg, wukv]
    rope = rope_tabs is not None
    if rope:
        nrb = rope_tabs[0].shape[0] // tm
        in_specs += [pl.BlockSpec((tm, HEAD_PAD), lambda i: (i % nrb, 0))] * 3
        args += list(rope_tabs)
    nh = MLA_HEADS * HEAD_PAD
    out_shape = [jax.ShapeDtypeStruct((t, nh), bf16)] * 4 + [
        jax.ShapeDtypeStruct((t, HEAD_PAD), bf16), jax.ShapeDtypeStruct((t, KV_LORA), f32),
        jax.ShapeDtypeStruct((t, HEAD_PAD), f32)]
    out_specs = [row(nh)] * 4 + [row(HEAD_PAD), row(KV_LORA), row(HEAD_PAD)]
    return pl.pallas_call(
        functools.partial(_mla_proj_kernel, rope=rope),
        grid=(t // tm,), in_specs=in_specs, out_specs=out_specs, out_shape=out_shape,
        compiler_params=_cparams("arbitrary"), name="mla_project",
    )(*args)


def _mm_kernel(a_ref, w_ref, o_ref):
    o_ref[...] = _dot(a_ref[...], w_ref[...]).astype(o_ref.dtype)


def matmul_bf16(a, w, tm):
    m, k = a.shape
    n = w.shape[1]
    return pl.pallas_call(
        _mm_kernel, grid=(m // tm,),
        in_specs=[pl.BlockSpec((tm, k), lambda i: (i, 0)), pl.BlockSpec((k, n), lambda i: (0, 0))],
        out_specs=pl.BlockSpec((tm, n), lambda i: (i, 0)),
        out_shape=jax.ShapeDtypeStruct((m, n), bf16),
        compiler_params=_cparams("arbitrary"), name="matmul_bf16",
    )(a, w)


def _attn_kernel(qn_ref, qr_ref, kn_ref, kr_ref, v_ref, o_ref, kcat_ref):
    nb = qn_ref.shape[0]

    @pl.when(pl.program_id(1) == 0)
    def _():
        for b in range(nb):
            for h in range(MLA_HEADS):
                kcat_ref[b, h, :, :HEAD_PAD] = kn_ref[b, :, h * HEAD_PAD:(h + 1) * HEAD_PAD]
                kcat_ref[b, h, :, HEAD_PAD:] = kr_ref[b]

    for b in range(nb):
        for h in range(MLA_HEADS):
            sl = slice(h * HEAD_PAD, (h + 1) * HEAD_PAD)
            q = jnp.concatenate([qn_ref[b, :, sl], qr_ref[b, :, sl]], axis=1)
            s = _dot_nt(q, kcat_ref[b, h]) * ATTN_SCALE
            e = jnp.exp(s - jnp.max(s, axis=-1, keepdims=True))
            l = jnp.sum(e, axis=-1, keepdims=True)
            o = _dot(e.astype(bf16), v_ref[b, :, sl]) / l
            o_ref[b, :, sl] = o.astype(bf16)


def mla_attention(qn, qr, kn, kr, v, tq=256, nb=1):
    b, lq, nh = qn.shape
    s = kn.shape[1]
    qspec = pl.BlockSpec((nb, tq, nh), lambda i, j: (i, j, 0))
    kspec = pl.BlockSpec((nb, s, nh), lambda i, j: (i, 0, 0))
    return pl.pallas_call(
        _attn_kernel, grid=(b // nb, lq // tq),
        in_specs=[qspec, qspec, kspec, pl.BlockSpec((nb, s, HEAD_PAD), lambda i, j: (i, 0, 0)), kspec],
        out_specs=qspec, out_shape=jax.ShapeDtypeStruct((b, lq, nh), bf16),
        scratch_shapes=[pltpu.VMEM((nb, MLA_HEADS, s, 2 * HEAD_PAD), bf16)],
        compiler_params=_cparams("arbitrary", "arbitrary"), name="mla_attention",
    )(qn, qr, kn, kr, v)


def _mm_res_kernel(a_ref, w_ref, x_ref, mod_ref, o_ref, *, kgate):
    y = _dot(a_ref[...], w_ref[...])
    o_ref[...] = x_ref[...] + mod_ref[0][kgate:kgate + 1, :] * y


def matmul_residual(a, w, x, mod, kgate, rows_per_mod, tm=512):
    t, k = a.shape
    return pl.pallas_call(
        functools.partial(_mm_res_kernel, kgate=kgate), grid=(t // tm,),
        in_specs=[pl.BlockSpec((tm, k), lambda i: (i, 0)), pl.BlockSpec((k, D), lambda i: (0, 0)),
                  pl.BlockSpec((tm, D), lambda i: (i, 0)),
                  pl.BlockSpec((1, N_MOD, D), lambda i: ((i * tm) // rows_per_mod, 0, 0))],
        out_specs=pl.BlockSpec((tm, D), lambda i: (i, 0)),
        out_shape=jax.ShapeDtypeStruct((t, D), f32),
        compiler_params=_cparams("arbitrary"), name="matmul_residual",
    )(a, w, x, mod)


INPROJ_BLOCK = 512
INPROJ_Z_BLOCKS = D_INNER // INPROJ_BLOCK
INPROJ_CONV_BLOCKS = CONV_DIM // INPROJ_BLOCK
DT_PAD = 128


def _inproj_kernel(x_ref, g_ref, mod_ref, w_ref, cw_ref, b_ref, z_ref, xbc_ref, dt_ref, h_ref, *, period):
    j = pl.program_id(1)

    @pl.when(j == 0)
    def _():
        h_ref[...] = _norm_mod(x_ref[...], g_ref[...], mod_ref[0], SHIFT1, SCALE1).astype(bf16)

    @pl.when(j < INPROJ_Z_BLOCKS)
    def _():
        z_ref[...] = _dot(h_ref[...], w_ref[...])

    @pl.when((j >= INPROJ_Z_BLOCKS) & (j < INPROJ_Z_BLOCKS + INPROJ_CONV_BLOCKS))
    def _():
        y = _dot(h_ref[...], w_ref[...])
        tm = y.shape[0]
        pos = lax.broadcasted_iota(i32, (tm, 1), 0) & (period - 1)
        prev = jnp.where(pos == 0, 0.0, pltpu.roll(y, 1, axis=0))
        nxt = jnp.where(pos == period - 1, 0.0, pltpu.roll(y, tm - 1, axis=0))
        cw = cw_ref[...]
        xbc_ref[...] = _silu(cw[0:1, :] * prev + cw[1:2, :] * y + cw[2:3, :] * nxt + b_ref[...])

    @pl.when(j == INPROJ_Z_BLOCKS + INPROJ_CONV_BLOCKS)
    def _():
        y = _dot(h_ref[...], w_ref[:, :DT_PAD]) + b_ref[:, :DT_PAD]
        dt_ref[...] = jnp.maximum(y, 0.0) + jnp.log1p(jnp.exp(-jnp.abs(y)))


def ssm_in_proj(x, g, mod, w_all, conv_w_all, bias_all, rows_per_mod, period, tm=2048):
    t = x.shape[0]
    nb = INPROJ_BLOCK
    nz, nx = INPROJ_Z_BLOCKS, INPROJ_CONV_BLOCKS
    col = lambda rows: pl.BlockSpec((rows, nb), lambda i, j: (0, j))
    return pl.pallas_call(
        functools.partial(_inproj_kernel, period=period), grid=(t // tm, nz + nx + 1),
        in_specs=[pl.BlockSpec((tm, D), lambda i, j: (i, 0)), pl.BlockSpec((1, D), lambda i, j: (0, 0)),
                  pl.BlockSpec((1, N_MOD, D), lambda i, j: ((i * tm) // rows_per_mod, 0, 0)),
                  col(D), col(3), col(1)],
        out_specs=[pl.BlockSpec((tm, nb), lambda i, j: (i, jnp.minimum(j, nz - 1))),
                   pl.BlockSpec((tm, nb), lambda i, j: (i, jnp.clip(j - nz, 0, nx - 1))),
                   pl.BlockSpec((tm, DT_PAD), lambda i, j: (i, 0))],
        out_shape=[jax.ShapeDtypeStruct((t, D_INNER), f32), jax.ShapeDtypeStruct((t, CONV_DIM), f32),
                   jax.ShapeDtypeStruct((t, DT_PAD), f32)],
        scratch_shapes=[pltpu.VMEM((tm, D), bf16)],
        compiler_params=_cparams("arbitrary", "arbitrary"), name="ssm_in_proj",
    )(x, g, mod, w_all, conv_w_all, bias_all)


def _ssd_kernel(*refs, nc, has_h0, has_hout, n_prev):
    xbc_ref, z_ref, dt_ref, dtT_ref, bT_ref, alr_ref, alc_ref, dsk_ref, ng_ref = refs[:9]
    rest = list(refs[9:])
    h0_ref = rest.pop(0) if has_h0 else None
    prev_ref = rest.pop(0) if n_prev else None
    y_ref = rest.pop(0)
    hout_ref = rest.pop(0) if has_hout else None
    state_ref, yf_ref, yc_ref, col_ref, row_ref = rest
    q = CHUNK
    d = pl.program_id(1)
    c = pl.program_id(2)
    ce = jnp.where(d == 0, c, nc - 1 - c)

    @pl.when(c == 0)
    def _():
        if has_h0:
            state_ref[...] = h0_ref[0, 0]
        else:
            state_ref[...] = jnp.zeros_like(state_ref)

    ii = lax.broadcasted_iota(i32, (q, q), 0)
    jj = lax.broadcasted_iota(i32, (q, q), 1)
    ahead = (ii - jj) * jnp.where(d == 0, 1, -1)
    causal = ahead >= 0
    first_head = lax.broadcasted_iota(i32, (1, 2 * SSM_P), 1) < SSM_P
    m_col = jnp.where(causal, 1.0, 0.0).astype(f32)
    m_row = jnp.where(ahead <= 0, 1.0, 0.0).astype(f32)

    a_row = -jnp.exp(alr_ref[0])
    a_col = -jnp.exp(alc_ref[0])
    dt = dt_ref[0, 0]
    dtT = dtT_ref[0, 0]
    daT = dtT * a_col
    cs_col = _dot_f32(m_col, dt * a_row)
    cs_row = _dot_f32(daT, m_row)
    tot = jnp.sum(daT, axis=1, keepdims=True)
    col_ref[0] = cs_col
    col_ref[1] = jnp.exp(cs_col)
    row_ref[0] = cs_row
    row_ref[1] = dtT
    row_ref[2] = jnp.exp(tot - cs_row) * dtT
    row_ref[3] = jnp.broadcast_to(jnp.exp(tot), (SSM_H, q))

    for g in range(SSM_G):
        b_g = xbc_ref[0, :, D_INNER + g * SSM_N: D_INNER + (g + 1) * SSM_N].astype(bf16)
        c_off = D_INNER + SSM_G * SSM_N
        c_g = xbc_ref[0, :, c_off + g * SSM_N: c_off + (g + 1) * SSM_N].astype(bf16)
        cb = _dot_nt(c_g, b_g)
        bT_g = bT_ref[0, g * SSM_N:(g + 1) * SSM_N, :]
        for pr in range(HEADS_PER_GROUP // 2):
            h0 = g * HEADS_PER_GROUP + 2 * pr
            h1 = h0 + 1
            ps = slice(h0 * SSM_P, (h1 + 1) * SSM_P)
            x2 = xbc_ref[0, :, ps]
            x_blk = jnp.concatenate([jnp.where(first_head, x2, 0.0), jnp.where(first_head, 0.0, x2)], axis=0).astype(bf16)
            st = state_ref[:, ps]
            ws, bs = [], []
            for h in (h0, h1):
                seg = col_ref[0, :, h:h + 1] - row_ref[0, h:h + 1, :]
                w = cb * jnp.exp(jnp.where(causal, seg, -jnp.inf)) * row_ref[1, h:h + 1, :]
                ws.append(w.astype(bf16))
                bs.append((bT_g * row_ref[2, h:h + 1, :]).astype(bf16))
            e_cs = jnp.where(first_head, col_ref[1, :, h0:h0 + 1], col_ref[1, :, h1:h1 + 1])
            yc_ref[:, ps] = _dot(jnp.concatenate(ws, axis=1), x_blk) + _dot(c_g, st.astype(bf16)) * e_cs
            dec = jnp.where(first_head, row_ref[3, h0:h0 + 1, :], row_ref[3, h1:h1 + 1, :])
            state_ref[:, ps] = st * dec + _dot(jnp.concatenate(bs, axis=1), x_blk)

    rows = pl.ds(pl.multiple_of(ce * q, q), q)

    @pl.when(d == 0)
    def _():
        yf_ref[rows, :] = yc_ref[...]

    @pl.when(d == 1)
    def _():
        y = yf_ref[rows, :] + yc_ref[...] + dsk_ref[...] * xbc_ref[0, :, :D_INNER]
        y = y * _silu(z_ref[0])
        y_ref[0] = _rms(y, ng_ref[...]).astype(bf16)

    if has_hout:
        @pl.when(c == nc - 1)
        def _():
            if n_prev:
                hout_ref[0, :n_prev, 0] = prev_ref[0, :, 0]
            hout_ref[0, n_prev, 0] = state_ref[...].T


def ssd_scan(xbc, z, dt_dir, dtT_dir, bT, a_log, d_skip_row, norm_g, h0, want_state=False, prev_states=None):
    b, l, _ = xbc.shape
    nc = l // CHUNK
    q = CHUNK
    has_h0 = h0 is not None
    has_hout = want_state
    n_prev = 0 if prev_states is None else prev_states.shape[1]
    ce = lambda d, c: jnp.where(d == 0, c, nc - 1 - c)
    late = lambda d, c: jnp.where(d == 0, nc - 1, nc - 1 - c)
    in_specs = [
        pl.BlockSpec((1, q, CONV_DIM), lambda i, d, c: (i, ce(d, c), 0)),
        pl.BlockSpec((1, q, D_INNER), lambda i, d, c: (i, late(d, c), 0)),
        pl.BlockSpec((1, 1, q, SSM_H), lambda i, d, c: (d, i, ce(d, c), 0)),
        pl.BlockSpec((1, 1, SSM_H, q), lambda i, d, c: (d, i, 0, ce(d, c))),
        pl.BlockSpec((1, SSM_G * SSM_N, q), lambda i, d, c: (i, 0, ce(d, c))),
        pl.BlockSpec((1, 1, SSM_H), lambda i, d, c: (d, 0, 0)),
        pl.BlockSpec((1, SSM_H, 1), lambda i, d, c: (d, 0, 0)),
        pl.BlockSpec((1, D_INNER), lambda i, d, c: (0, 0)),
        pl.BlockSpec((1, D_INNER), lambda i, d, c: (0, 0)),
    ]
    args = [xbc, z, dt_dir, dtT_dir, bT, a_log.reshape(2, 1, SSM_H), a_log.reshape(2, SSM_H, 1), d_skip_row, norm_g]
    if has_h0:
        in_specs.append(pl.BlockSpec((1, 1, SSM_N, D_INNER), lambda i, d, c: (i, d, 0, 0)))
        args.append(h0)
    out_specs = [pl.BlockSpec((1, q, D_INNER), lambda i, d, c: (i, late(d, c), 0))]
    out_shape = [jax.ShapeDtypeStruct((b, l, D_INNER), bf16)]
    if has_hout:
        if n_prev:
            in_specs.append(pl.BlockSpec((1, n_prev, 1, D_INNER, SSM_N), lambda i, d, c: (i, 0, d, 0, 0)))
            args.append(prev_states)
        out_specs.append(pl.BlockSpec((1, n_prev + 1, 1, D_INNER, SSM_N), lambda i, d, c: (i, 0, d, 0, 0)))
        out_shape.append(jax.ShapeDtypeStruct((b, n_prev + 1, 2, D_INNER, SSM_N), f32))
    return pl.pallas_call(
        functools.partial(_ssd_kernel, nc=nc, has_h0=has_h0, has_hout=has_hout, n_prev=n_prev),
        grid=(b, 2, nc),
        in_specs=in_specs, out_specs=out_specs, out_shape=out_shape,
        scratch_shapes=[
            pltpu.VMEM((SSM_N, D_INNER), f32),
            pltpu.VMEM((l, D_INNER), f32),
            pltpu.VMEM((q, D_INNER), f32),
            pltpu.VMEM((2, q, SSM_H), f32),
            pltpu.VMEM((4, SSM_H, q), f32),
        ],
        compiler_params=_cparams("arbitrary", "arbitrary", "arbitrary"), name="ssd_scan",
    )(*args)


def _router_kernel(x_ref, g_ref, mod_ref, wr_ref, h_ref, aff_ref):
    h = _norm_mod(x_ref[...], g_ref[...], mod_ref[0], SHIFT2, SCALE2)
    hb = h.astype(bf16)
    h_ref[...] = hb
    h_lo = (h - hb.astype(f32)).astype(bf16)
    w = wr_ref[...]
    w_hi = w.astype(bf16)
    w_lo = (w - w_hi.astype(f32)).astype(bf16)
    lg = _dot_nt(w_hi, hb) + _dot_nt(w_hi, h_lo) + _dot_nt(w_lo, hb)
    e = jnp.exp(lg - jnp.max(lg, axis=0, keepdims=True))
    aff_ref[...] = e / jnp.sum(e, axis=0, keepdims=True)


def moe_router(x, g, mod, w_router_t, rows_per_mod, tm=512):
    t = x.shape[0]
    return pl.pallas_call(
        _router_kernel, grid=(t // tm,),
        in_specs=[pl.BlockSpec((tm, D), lambda i: (i, 0)), pl.BlockSpec((1, D), lambda i: (0, 0)),
                  pl.BlockSpec((1, N_MOD, D), lambda i: ((i * tm) // rows_per_mod, 0, 0)),
                  pl.BlockSpec((N_EXPERTS, D), lambda i: (0, 0))],
        out_specs=[pl.BlockSpec((tm, D), lambda i: (i, 0)), pl.BlockSpec((N_EXPERTS, tm), lambda i: (0, i))],
        out_shape=[jax.ShapeDtypeStruct((t, D), bf16), jax.ShapeDtypeStruct((N_EXPERTS, t), f32)],
        compiler_params=_cparams("arbitrary"), name="moe_router",
    )(x, g, mod, w_router_t)


THRESHOLD_REFINE_STEPS = 28
META_HI = 8
META_NROUND = 64


def _route_kernel(aff_ref, key_ref, start_ref, meta_ref, *, cap, window):
    t = aff_ref.shape[1]
    aff = aff_ref[...]

    def count_ge(v):
        return jnp.sum(jnp.where(aff >= v, 1.0, 0.0), axis=1, keepdims=True)

    bits = lax.bitcast_convert_type(aff, i32)
    tb = jnp.zeros((N_EXPERTS, 1), i32)
    for bit in range(30, -1, -1):
        cand = tb | (1 << bit)
        cnt = jnp.sum((bits >= cand).astype(i32), axis=1, keepdims=True)
        tb = jnp.where(cnt >= cap, cand, tb)
    approx = lax.bitcast_convert_type(tb, f32)
    lo = jnp.where(count_ge(0.5 * approx) >= cap, 0.5 * approx, 0.0)
    hi = jnp.where(count_ge(2.0 * approx) < cap, 2.0 * approx, 2.0)

    def refine(_, lh):
        lo, hi = lh
        mid = 0.5 * (lo + hi)
        ge = count_ge(mid) >= cap
        return jnp.where(ge, mid, lo), jnp.where(ge, hi, mid)

    thr, _ = lax.fori_loop(0, THRESHOLD_REFINE_STEPS, refine, (lo, hi))
    gt = aff > thr
    eq = aff == thr
    gtf = jnp.where(gt, 1.0, 0.0)
    eqf = jnp.where(eq, 1.0, 0.0)
    need = cap - jnp.sum(gtf, axis=1, keepdims=True).astype(i32)
    blk = TOKEN_BLOCK
    tri = (lax.broadcasted_iota(i32, (blk, blk), 0) <= lax.broadcasted_iota(i32, (blk, blk), 1))
    tri = jnp.where(tri, 1.0, 0.0).astype(bf16)
    carry = jnp.zeros((2 * N_EXPERTS, 1), f32)
    start_ref[...] = jnp.zeros_like(start_ref)
    meta_ref[...] = jnp.zeros_like(meta_ref)
    before = jnp.zeros((N_EXPERTS, 1), i32)
    nsub = cap // GATHER_TILE
    assert nsub <= META_HI
    first_blk = [jnp.zeros((N_EXPERTS, 1), i32) for _ in range(nsub)]
    last_blk = [jnp.zeros((N_EXPERTS, 1), i32) for _ in range(nsub)]
    for j in range(t // blk):
        sl = slice(j * blk, (j + 1) * blk)
        m = jnp.concatenate([gtf[:, sl], eqf[:, sl]], axis=0)
        pc = _dot(m.astype(bf16), tri) + carry
        carry = pc[:, blk - 1:blk]
        cs_gt = pc[:N_EXPERTS].astype(i32)
        cs_eq = pc[N_EXPERTS:].astype(i32)
        sel = gt[:, sl] | (eq[:, sl] & (cs_eq <= need))
        cs = cs_gt + jnp.minimum(cs_eq, need)
        key_ref[:, sl] = jnp.where(sel, cs, 0)
        end = cs[:, blk - 1:blk]
        start = before & ~7
        start_ref[:, j:j + 1] = start
        rounds = (end - start + (window - 1)) // window
        meta_ref[:, META_NROUND + j:META_NROUND + j + 1] = jnp.broadcast_to(
            jnp.max(rounds, axis=0, keepdims=True), (N_EXPERTS, 1))
        for s in range(nsub):
            first_blk[s] = first_blk[s] + jnp.where(end <= s * GATHER_TILE, 1, 0)
            last_blk[s] = last_blk[s] + jnp.where(end < (s + 1) * GATHER_TILE, 1, 0)
        before = end
    for s in range(nsub):
        meta_ref[:, s:s + 1] = first_blk[s]
        meta_ref[:, META_HI + s:META_HI + s + 1] = last_blk[s]


def moe_route(aff_t, cap, window):
    t = aff_t.shape[1]
    small = jax.ShapeDtypeStruct((N_EXPERTS, 128), i32)
    small_spec = pl.BlockSpec((N_EXPERTS, 128), lambda i: (0, 0))
    return pl.pallas_call(
        functools.partial(_route_kernel, cap=cap, window=window), grid=(1,),
        in_specs=[pl.BlockSpec((N_EXPERTS, t), lambda i: (0, 0))],
        out_specs=[pl.BlockSpec((N_EXPERTS, t), lambda i: (0, 0)), small_spec, small_spec],
        out_shape=[jax.ShapeDtypeStruct((N_EXPERTS, t), i32), small, small],
        compiler_params=_cparams("arbitrary"), name="moe_route",
    )(aff_t)


def _moe_ffn_kernel(meta_ref, key_ref, aff_ref, h_ref, wg_ref, wu_ref, wd_ref, o_ref, acc_ref, gate_ref):
    e = pl.program_id(0)
    j = pl.program_id(1)
    tm = acc_ref.shape[0]
    tb = TOKEN_BLOCK
    acc_ref[...] = jnp.zeros_like(acc_ref)
    gate_ref[...] = jnp.zeros_like(gate_ref)
    gt = GATHER_TILE
    nsub = tm // gt
    nblk = key_ref.shape[1]
    gb = GATHER_BLOCKS
    row_id = lax.broadcasted_iota(i32, (gt, 1), 0)
    for s in range(nsub):
        sub = j * nsub + s
        srows = slice(s * gt, (s + 1) * gt)
        first = meta_ref[e, sub]
        last = meta_ref[e, META_HI + sub]

        def window(w, carry, sub=sub, srows=srows, first=first):
            begin = first + w * gb
            start = jnp.minimum(begin, nblk - gb)
            pieces = []
            gate = jnp.zeros((gt, 1), f32)
            for k in range(gb):
                b = start + k
                slot = jnp.where(b >= begin, sub * gt + 1, -gt) + row_id
                oh = key_ref[0, b] == slot
                pieces.append(jnp.where(oh, 1.0, 0.0).astype(bf16))
                gate = gate + jnp.sum(jnp.where(oh, aff_ref[0, b], 0.0), axis=1, keepdims=True)
            rows = pl.ds(pl.multiple_of(start * tb, tb), gb * tb)
            acc_ref[srows, :] += _dot(jnp.concatenate(pieces, axis=1), h_ref[rows, :])
            gate_ref[srows, :] += gate
            return carry

        lax.fori_loop(0, (last - first) // gb + 1, window, 0)
    xe = acc_ref[...].astype(bf16)
    hid = _silu(_dot(xe, wg_ref[0, 0].astype(bf16))) * _dot(xe, wu_ref[0, 0].astype(bf16))
    o_ref[0] = _dot(hid.astype(bf16), wd_ref[0, 0].astype(bf16)) * gate_ref[...]


def moe_ffn(meta, key, aff_t, h, wg, wu, wd, layer, cap):
    t = h.shape[0]
    tm = SLOT_TILE
    nblk = t // TOKEN_BLOCK
    nj = cap // tm
    key4 = key.reshape(N_EXPERTS, nblk, 1, TOKEN_BLOCK)
    aff4 = aff_t.reshape(N_EXPERTS, nblk, 1, TOKEN_BLOCK)
    row_spec = pl.BlockSpec((1, nblk, 1, TOKEN_BLOCK), lambda e, j, *_: (e, 0, 0, 0))
    wspec = pl.BlockSpec((1, 1, D, D_FF), lambda e, j, *_: (layer, e, 0, 0))
    grid_spec = pltpu.PrefetchScalarGridSpec(
        num_scalar_prefetch=1, grid=(N_EXPERTS, nj),
        in_specs=[row_spec, row_spec,
                  pl.BlockSpec((t, D), lambda e, j, *_: (0, 0), pipeline_mode=pl.Buffered(1)),
                  wspec, wspec, pl.BlockSpec((1, 1, D_FF, D), lambda e, j, *_: (layer, e, 0, 0))],
        out_specs=pl.BlockSpec((1, tm, D), lambda e, j, *_: (e, j, 0)),
        scratch_shapes=[pltpu.VMEM((tm, D), f32), pltpu.VMEM((tm, 1), f32)])
    return pl.pallas_call(
        _moe_ffn_kernel, grid_spec=grid_spec,
        out_shape=jax.ShapeDtypeStruct((N_EXPERTS, cap, D), f32),
        compiler_params=_cparams("arbitrary", "arbitrary"), name="moe_ffn",
    )(meta, key4, aff4, h, wg, wu, wd)


def _moe_combine_kernel(start_ref, meta_ref, key_ref, ye_hbm, x_ref, mod_ref, o_ref, stage_ref, acc_ref, sem,
                        *, cap, nblk, window):
    i = pl.program_id(0)
    w = window

    def window_start(tile, e, r):
        return start_ref[e, tile] + r * w

    def copies(tile, r, buf):
        out = []
        for e in range(N_EXPERTS):
            first = pl.multiple_of(jnp.minimum(window_start(tile, e, r), cap - w), 8)
            out.append(pltpu.make_async_copy(ye_hbm.at[e, pl.ds(first, w), :],
                                             stage_ref.at[buf, pl.ds(e * w, w), :], sem.at[buf]))
        return out

    def accumulate(r, buf):
        pieces = []
        for e in range(N_EXPERTS):
            begin = window_start(i, e, r)
            slot = jnp.minimum(begin, cap - w) + lax.broadcasted_iota(i32, (w, 1), 0) + 1
            oh = (key_ref[e:e + 1, :] == slot) & (slot > begin)
            pieces.append(jnp.where(oh, 1.0, 0.0))
        oh = jnp.concatenate(pieces, axis=0).astype(bf16)
        rows = stage_ref[buf]
        hi = rows.astype(bf16)
        lo = (rows - hi.astype(f32)).astype(bf16)
        acc_ref[...] += _dot_tn(oh, hi) + _dot_tn(oh, lo)

    cur = lax.rem(i, 2)

    @pl.when(i == 0)
    def _():
        for cp in copies(0, 0, 0):
            cp.start()

    @pl.when(i + 1 < nblk)
    def _():
        for cp in copies(i + 1, 0, 1 - cur):
            cp.start()

    acc_ref[...] = jnp.zeros_like(acc_ref)
    for cp in copies(i, 0, cur):
        cp.wait()
    accumulate(0, cur)

    def extra_round(r, carry):
        cps = copies(i, r, 2)
        for cp in cps:
            cp.start()
        for cp in cps:
            cp.wait()
        accumulate(r, 2)
        return carry

    lax.fori_loop(1, meta_ref[0, META_NROUND + i], extra_round, 0)
    o_ref[...] = x_ref[...] + mod_ref[0][GATE2:GATE2 + 1, :] * acc_ref[...]


def moe_combine(starts, meta, key, ye, x, mod, cap, rows_per_mod, window):
    t = x.shape[0]
    tb = TOKEN_BLOCK
    nblk = t // tb
    grid_spec = pltpu.PrefetchScalarGridSpec(
        num_scalar_prefetch=2, grid=(nblk,),
        in_specs=[pl.BlockSpec((N_EXPERTS, tb), lambda i, *_: (0, i)),
                  pl.BlockSpec(memory_space=pl.ANY),
                  pl.BlockSpec((tb, D), lambda i, *_: (i, 0)),
                  pl.BlockSpec((1, N_MOD, D), lambda i, *_: ((i * tb) // rows_per_mod, 0, 0))],
        out_specs=pl.BlockSpec((tb, D), lambda i, *_: (i, 0)),
        scratch_shapes=[pltpu.VMEM((3, N_EXPERTS * window, D), f32), pltpu.VMEM((tb, D), f32),
                        pltpu.SemaphoreType.DMA((3,))])
    return pl.pallas_call(
        functools.partial(_moe_combine_kernel, cap=cap, nblk=nblk, window=window), grid_spec=grid_spec,
        out_shape=jax.ShapeDtypeStruct((t, D), f32),
        compiler_params=_cparams("arbitrary"), name="moe_combine",
    )(starts, meta, key, ye, x, mod)


def moe_layer(x, g2, mod, w_router_t, wg, wu, wd, layer, rows_per_mod, window):
    t = x.shape[0]
    cap = 2 * t // N_EXPERTS
    h, aff_t = moe_router(x, g2, mod, w_router_t, rows_per_mod)
    key, starts, meta = moe_route(aff_t, cap, window)
    ye = moe_ffn(meta, key, aff_t, h, wg, wu, wd, layer, cap)
    return moe_combine(starts, meta, key, ye, x, mod, cap, rows_per_mod, window)


def _final_norm_kernel(x_ref, g_ref, o_ref):
    o_ref[...] = _rms(x_ref[...], g_ref[...])


def final_norm(x, g, tm=1024):
    t = x.shape[0]
    return pl.pallas_call(
        _final_norm_kernel, grid=(t // tm,),
        in_specs=[pl.BlockSpec((tm, D), lambda i: (i, 0)), pl.BlockSpec((1, D), lambda i: (0, 0))],
        out_specs=pl.BlockSpec((tm, D), lambda i: (i, 0)),
        out_shape=jax.ShapeDtypeStruct((t, D), f32),
        compiler_params=_cparams("arbitrary"), name="final_norm",
    )(x, g)


def _rope_tables(seq):
    rows = seq // GRID_W
    row = jnp.repeat(jnp.arange(rows), GRID_W).astype(f32)
    col = jnp.tile(jnp.arange(GRID_W), rows).astype(f32)
    pairs = QK_ROPE // 4
    inv = ROPE_THETA ** (-jnp.arange(pairs, dtype=f32) / pairs)
    ang = jnp.concatenate([row[:, None] * inv, col[:, None] * inv], axis=-1)
    cos, sin = jnp.cos(ang), jnp.sin(ang)
    zero = jnp.zeros_like(cos)
    c = jnp.concatenate([cos, cos, zero, zero], axis=-1)
    s1 = jnp.concatenate([-sin, zero, zero, zero], axis=-1)
    s2 = jnp.concatenate([zero, sin, zero, zero], axis=-1)
    return c, s1, s2


def _mla_weights(w_dq, q_norm_g, w_uq, w_dkv, kv_norm_g, w_ukv, w_o):
    per_head = QK_NOPE + QK_ROPE
    uq = w_uq.reshape(Q_LORA, MLA_HEADS, per_head)
    uq_nope = uq[..., :QK_NOPE].reshape(Q_LORA, MLA_HEADS * QK_NOPE)
    uq_rope = jnp.pad(uq[..., QK_NOPE:], ((0, 0), (0, 0), (0, HEAD_PAD - QK_ROPE))).reshape(Q_LORA, MLA_HEADS * HEAD_PAD)
    wuq = jnp.concatenate([uq_nope, uq_rope], axis=1).astype(bf16)
    wdkv = jnp.pad(w_dkv, ((0, 0), (0, HEAD_PAD - QK_ROPE))).astype(bf16)
    ukv = w_ukv.reshape(KV_LORA, MLA_HEADS, QK_NOPE + V_HEAD)
    wukv = jnp.concatenate([ukv[..., :QK_NOPE].reshape(KV_LORA, -1), ukv[..., QK_NOPE:].reshape(KV_LORA, -1)], axis=1).astype(bf16)
    proj = (w_dq.astype(bf16), q_norm_g.reshape(1, Q_LORA), wuq, wdkv, kv_norm_g.reshape(1, KV_LORA), wukv)
    return proj, wukv, w_o.astype(bf16)


def _mla_layer(xp, xs, modp, mods, g1, cache_ckv_j, cache_krope_j, w, rope_tabs):
    proj_w, wukv, w_o = w
    bp, bs = xp.shape[0] // 256, xs.shape[0] // 2048
    nh = MLA_HEADS * HEAD_PAD
    qn, qr, kn, v, kr, ckv, krope = mla_project(xp, g1, modp, proj_w, None, xp.shape[0])
    r3 = lambda a, b: a.reshape(b, -1, a.shape[-1])
    op = mla_attention(r3(qn, bp), r3(qr, bp), r3(kn, bp), r3(kr, bp), r3(v, bp), nb=2).reshape(-1, nh)
    xp = matmul_residual(op, w_o, xp, modp, GATE1, xp.shape[0])
    new_ckv = ckv.reshape(bp, -1, KV_LORA)
    new_krope = krope[:, :QK_ROPE].reshape(bp, -1, QK_ROPE)
    qn, qr, kn, v, kr, _, _ = mla_project(xs, g1, mods, proj_w, rope_tabs, 2048)
    ctx = matmul_bf16(cache_ckv_j.reshape(-1, KV_LORA).astype(bf16), wukv, tm=512)
    nk = MLA_HEADS * QK_NOPE
    kn_all = jnp.concatenate([ctx[:, :nk].reshape(bs, -1, nk), r3(kn, bs)], axis=1)
    v_all = jnp.concatenate([ctx[:, nk:].reshape(bs, -1, nk), r3(v, bs)], axis=1)
    kr_ctx = jnp.pad(cache_krope_j, ((0, 0), (0, 0), (0, HEAD_PAD - QK_ROPE))).astype(bf16)
    kr_all = jnp.concatenate([kr_ctx, r3(kr, bs)], axis=1)
    os_ = mla_attention(r3(qn, bs), r3(qr, bs), kn_all, kr_all, v_all).reshape(-1, nh)
    xs = matmul_residual(os_, w_o, xs, mods, GATE1, 2048)
    return xp, xs, new_ckv, new_krope


def _ssm_stream(x, mod, g1, w, rows_per_mod, seq, h0, want_state=False, prev_states=None):
    w_all, conv_w_all, bias_all, a_log, d_skip_row, norm_g, w_out = w
    b = x.shape[0] // seq
    z, xbc, dt = ssm_in_proj(x, g1, mod, w_all, conv_w_all, bias_all, rows_per_mod, seq)
    dt4 = dt[:, :2 * SSM_H].reshape(b, seq, 2, SSM_H)
    dt_dir = dt4.transpose(2, 0, 1, 3)
    dtT_dir = dt4.transpose(2, 0, 3, 1)
    xbc3 = xbc.reshape(b, seq, CONV_DIM)
    bT = xbc3[:, :, D_INNER:D_INNER + SSM_G * SSM_N].transpose(0, 2, 1)
    outs = ssd_scan(xbc3, z.reshape(b, seq, D_INNER), dt_dir, dtT_dir, bT, a_log, d_skip_row, norm_g, h0,
                    want_state, prev_states)
    x = matmul_residual(outs[0].reshape(-1, D_INNER), w_out, x, mod, GATE1, rows_per_mod)
    return x, (outs[1] if want_state else None)


def kernel(x_prompt, x_sample, cache_ckv, cache_krope, state_ssm, c, c_ctx, w_mod, b_mod, norm1_g, norm2_g, final_norm_g, mla_w_dq, mla_q_norm_g, mla_w_uq, mla_w_dkv, mla_kv_norm_g, mla_w_ukv, mla_w_o, ssm_w_in, ssm_conv_w, ssm_conv_b, ssm_dt_bias, ssm_a_log, ssm_d_skip, ssm_norm_g, ssm_w_out, moe_w_router, moe_w_gate, moe_w_up, moe_w_down):
    bp, lp, _ = x_prompt.shape
    bs, ls, _ = x_sample.shape
    xp = x_prompt.reshape(bp * lp, D)
    xs = x_sample.reshape(bs * ls, D)

    cond8 = jnp.concatenate([c_ctx[None, :], c, jnp.zeros((8 - 1 - bs, D), f32)], axis=0)
    mod_all = modulation_all(cond8, w_mod, b_mod)
    rope_tabs = _rope_tables(ls)

    new_ckv, new_krope = [], []
    new_ssm = None
    for l in range(DEPTH):
        modp = mod_all[l, 0:1].reshape(1, N_MOD, D)
        mods = mod_all[l, 1:1 + bs].reshape(bs, N_MOD, D)
        g1 = norm1_g[l].reshape(1, D)
        j = l // 2
        if l % 2 == 0:
            w = _mla_weights(mla_w_dq[j], mla_q_norm_g[j], mla_w_uq[j], mla_w_dkv[j], mla_kv_norm_g[j], mla_w_ukv[j], mla_w_o[j])
            xp, xs, ckv, krope = _mla_layer(xp, xs, modp, mods, g1, cache_ckv[:, j], cache_krope[:, j], w, rope_tabs)
            new_ckv.append(ckv)
            new_krope.append(krope)
        else:
            dt_cols = INPROJ_BLOCK - 2 * SSM_H
            w_all = jnp.pad(ssm_w_in[j], ((0, 0), (0, dt_cols))).astype(bf16)
            conv_w_all = jnp.pad(ssm_conv_w[j].T, ((0, 0), (D_INNER, INPROJ_BLOCK)))
            bias_all = jnp.concatenate([jnp.zeros((D_INNER,), f32), ssm_conv_b[j], ssm_dt_bias[j].reshape(-1),
                                        jnp.zeros((dt_cols,), f32)]).reshape(1, -1)
            w = (w_all, conv_w_all, bias_all, ssm_a_log[j],
                 jnp.repeat(ssm_d_skip[j], SSM_P).reshape(1, D_INNER), ssm_norm_g[j].reshape(1, D_INNER),
                 ssm_w_out[j].astype(bf16))
            xp, new_ssm = _ssm_stream(xp, modp, g1, w, bp * lp, lp, None, want_state=True, prev_states=new_ssm)
            h0 = state_ssm[:, j].transpose(0, 1, 4, 2, 3).reshape(bs, 2, SSM_N, D_INNER)
            xs, _ = _ssm_stream(xs, mods, g1, w, ls, ls, h0)
        g2 = norm2_g[l].reshape(1, D)
        moe_w = (moe_w_router[l].T, moe_w_gate, moe_w_up, moe_w_down, l)
        xp = moe_layer(xp, g2, modp, *moe_w, bp * lp, COMBINE_WINDOW_PROMPT)
        xs = moe_layer(xs, g2, mods, *moe_w, ls, COMBINE_WINDOW_SAMPLE)

    fg = final_norm_g.reshape(1, D)
    y_prompt = final_norm(xp, fg).reshape(bp, lp, D)
    y_sample = final_norm(xs, fg).reshape(bs, ls, D)
    new_ssm = new_ssm.reshape(bp, DEPTH // 2, 2, SSM_H, SSM_P, SSM_N)
    return (y_prompt, y_sample, jnp.stack(new_ckv, axis=1), jnp.stack(new_krope, axis=1), new_ssm)
```

```python
import functools

import jax
import jax.numpy as jnp
from jax import lax
from jax.experimental import pallas as pl
from jax.experimental.pallas import tpu as pltpu

f32 = jnp.float32
bf16 = jnp.bfloat16
i32 = jnp.int32

D = 1024
DEPTH = 4
N_MOD = 6
EPS = 1e-6
GRID_W = 64

MLA_HEADS = 8
QK_NOPE = 128
QK_ROPE = 64
V_HEAD = 128
Q_LORA = 384
KV_LORA = 256
ROPE_THETA = 10000.0
ATTN_SCALE = (QK_NOPE + QK_ROPE) ** -0.5
HEAD_PAD = 128

D_INNER = 2 * D
SSM_P = 64
SSM_H = D_INNER // SSM_P
SSM_G = 4
SSM_N = 128
CHUNK = 128
CONV_DIM = D_INNER + 2 * SSM_G * SSM_N
HEADS_PER_GROUP = SSM_H // SSM_G
GROUP_W = HEADS_PER_GROUP * SSM_P

N_EXPERTS = 16
D_FF = 1024
TOKEN_BLOCK = 256
SLOT_TILE = 512
GATHER_TILE = 128
GATHER_BLOCKS = 8
COMBINE_WINDOW_PROMPT = 64
COMBINE_WINDOW_SAMPLE = 128

VMEM_LIMIT = 56 * 1024 * 1024

SHIFT1, SCALE1, GATE1, SHIFT2, SCALE2, GATE2 = range(6)


def _cparams(*sem):
    return pltpu.CompilerParams(dimension_semantics=sem, vmem_limit_bytes=VMEM_LIMIT)


def _dot(a, b):
    return jnp.dot(a, b, preferred_element_type=f32)


def _dot_nt(a, b):
    return lax.dot_general(a, b, (((1,), (1,)), ((), ())), preferred_element_type=f32)


def _dot_tn(a, b):
    return lax.dot_general(a, b, (((0,), (0,)), ((), ())), preferred_element_type=f32)


def _dot_f32(a, b):
    return jnp.dot(a, b, preferred_element_type=f32, precision=lax.Precision.HIGHEST)


def _rms(x, g):
    ms = jnp.mean(x * x, axis=-1, keepdims=True)
    return x * lax.rsqrt(ms + EPS) * g


def _norm_mod(x, g, mod, k_shift, k_scale):
    return _rms(x, g) * (1.0 + mod[k_scale:k_scale + 1, :]) + mod[k_shift:k_shift + 1, :]


def _silu(x):
    return x * jax.nn.sigmoid(x)


def _mod_kernel(c_ref, w_ref, b_ref, o_ref):
    s = _silu(c_ref[...]).astype(bf16)
    o_ref[0] = _dot(s, w_ref[0].astype(bf16)) + b_ref[0]


def modulation_all(cond8, w_mod, b_mod):
    nb = 1536
    n = N_MOD * D
    return pl.pallas_call(
        _mod_kernel,
        grid=(DEPTH, n // nb),
        in_specs=[
            pl.BlockSpec((8, D), lambda l, j: (0, 0)),
            pl.BlockSpec((1, D, nb), lambda l, j: (l, 0, j)),
            pl.BlockSpec((1, 1, nb), lambda l, j: (l, 0, j)),
        ],
        out_specs=pl.BlockSpec((1, 8, nb), lambda l, j: (l, 0, j)),
        out_shape=jax.ShapeDtypeStruct((DEPTH, 8, n), f32),
        compiler_params=_cparams("arbitrary", "arbitrary"),
        name="modulation",
    )(cond8, w_mod, b_mod.reshape(DEPTH, 1, n))


def _rope_rot(p, c, s1, s2):
    return p * c + pltpu.roll(p, 96, axis=1) * s1 + pltpu.roll(p, 32, axis=1) * s2


def _mla_proj_kernel(*refs, rope):
    if rope:
        (x_ref, g_ref, mod_ref, wdq_ref, qg_ref, wuq_ref, wdkv_ref, kvg_ref, wukv_ref, rc_ref, rs1_ref, rs2_ref,
         qn_ref, qr_ref, kn_ref, v_ref, kr_ref, ckv_ref, krope_ref) = refs
    else:
        (x_ref, g_ref, mod_ref, wdq_ref, qg_ref, wuq_ref, wdkv_ref, kvg_ref, wukv_ref,
         qn_ref, qr_ref, kn_ref, v_ref, kr_ref, ckv_ref, krope_ref) = refs
    h = _norm_mod(x_ref[...], g_ref[...], mod_ref[0], SHIFT1, SCALE1).astype(bf16)
    nq = MLA_HEADS * QK_NOPE
    q_lat = _rms(_dot(h, wdq_ref[...]), qg_ref[...]).astype(bf16)
    q = _dot(q_lat, wuq_ref[...])
    qn_ref[...] = q[:, :nq].astype(bf16)
    if rope:
        c, s1, s2 = rc_ref[...], rs1_ref[...], rs2_ref[...]
    for hh in range(MLA_HEADS):
        piece = q[:, nq + hh * HEAD_PAD: nq + (hh + 1) * HEAD_PAD]
        if rope:
            piece = _rope_rot(piece, c, s1, s2)
        qr_ref[:, hh * HEAD_PAD:(hh + 1) * HEAD_PAD] = piece.astype(bf16)
    kv = _dot(h, wdkv_ref[...])
    ckv = _rms(kv[:, :KV_LORA], kvg_ref[...])
    ckv_ref[...] = ckv
    kr = kv[:, KV_LORA:]
    krope_ref[...] = kr
    if rope:
        kr = _rope_rot(kr, c, s1, s2)
    kr_ref[...] = kr.astype(bf16)
    kvx = _dot(ckv.astype(bf16), wukv_ref[...])
    nk = MLA_HEADS * QK_NOPE
    kn_ref[...] = kvx[:, :nk].astype(bf16)
    v_ref[...] = kvx[:, nk:].astype(bf16)


def mla_project(x, g, mod, w, rope_tabs, rows_per_mod, tm=256):
    t = x.shape[0]
    wdq, qg, wuq, wdkv, kvg, wukv = w
    full = lambda a: pl.BlockSpec(a.shape, lambda i: (0,) * a.ndim)
    row = lambda n: pl.BlockSpec((tm, n), lambda i: (i, 0))
    in_specs = [row(D), full(g), pl.BlockSpec((1, N_MOD, D), lambda i: ((i * tm) // rows_per_mod, 0, 0)),
                full(wdq), full(qg), full(wuq), full(wdkv), full(kvg), full(wukv)]
    args = [x, g, mod, wdq, qg, wuq, wdkv, kvg, wukv]
    rope = rope_tabs is not None
    if rope:
        nrb = rope_tabs[0].shape[0] // tm
        in_specs += [pl.BlockSpec((tm, HEAD_PAD), lambda i: (i % nrb, 0))] * 3
        args += list(rope_tabs)
    nh = MLA_HEADS * HEAD_PAD
    out_shape = [jax.ShapeDtypeStruct((t, nh), bf16)] * 4 + [
        jax.ShapeDtypeStruct((t, HEAD_PAD), bf16), jax.ShapeDtypeStruct((t, KV_LORA), f32),
        jax.ShapeDtypeStruct((t, HEAD_PAD), f32)]
    out_specs = [row(nh)] * 4 + [row(HEAD_PAD), row(KV_LORA), row(HEAD_PAD)]
    return pl.pallas_call(
        functools.partial(_mla_proj_kernel, rope=rope),
        grid=(t // tm,), in_specs=in_specs, out_specs=out_specs, out_shape=out_shape,
        compiler_params=_cparams("arbitrary"), name="mla_project",
    )(*args)


def _mm_kernel(a_ref, w_ref, o_ref):
    o_ref[...] = _dot(a_ref[...], w_ref[...]).astype(o_ref.dtype)


def matmul_bf16(a, w, tm):
    m, k = a.shape
    n = w.shape[1]
    return pl.pallas_call(
        _mm_kernel, grid=(m // tm,),
        in_specs=[pl.BlockSpec((tm, k), lambda i: (i, 0)), pl.BlockSpec((k, n), lambda i: (0, 0))],
        out_specs=pl.BlockSpec((tm, n), lambda i: (i, 0)),
        out_shape=jax.ShapeDtypeStruct((m, n), bf16),
        compiler_params=_cparams("arbitrary"), name="matmul_bf16",
    )(a, w)


def _attn_kernel(qn_ref, qr_ref, kn_ref, kr_ref, v_ref, o_ref, kcat_ref):
    nb = qn_ref.shape[0]

    @pl.when(pl.program_id(1) == 0)
    def _():
        for b in range(nb):
            for h in range(MLA_HEADS):
                kcat_ref[b, h, :, :HEAD_PAD] = kn_ref[b, :, h * HEAD_PAD:(h + 1) * HEAD_PAD]
                kcat_ref[b, h, :, HEAD_PAD:] = kr_ref[b]

    for b in range(nb):
        for h in range(MLA_HEADS):
            sl = slice(h * HEAD_PAD, (h + 1) * HEAD_PAD)
            q = jnp.concatenate([qn_ref[b, :, sl], qr_ref[b, :, sl]], axis=1)
            s = _dot_nt(q, kcat_ref[b, h]) * ATTN_SCALE
            e = jnp.exp(s - jnp.max(s, axis=-1, keepdims=True))
            l = jnp.sum(e, axis=-1, keepdims=True)
            o = _dot(e.astype(bf16), v_ref[b, :, sl]) / l
            o_ref[b, :, sl] = o.astype(bf16)


def mla_attention(qn, qr, kn, kr, v, tq=256, nb=1):
    b, lq, nh = qn.shape
    s = kn.shape[1]
    qspec = pl.BlockSpec((nb, tq, nh), lambda i, j: (i, j, 0))
    kspec = pl.BlockSpec((nb, s, nh), lambda i, j: (i, 0, 0))
    return pl.pallas_call(
        _attn_kernel, grid=(b // nb, lq // tq),
        in_specs=[qspec, qspec, kspec, pl.BlockSpec((nb, s, HEAD_PAD), lambda i, j: (i, 0, 0)), kspec],
        out_specs=qspec, out_shape=jax.ShapeDtypeStruct((b, lq, nh), bf16),
        scratch_shapes=[pltpu.VMEM((nb, MLA_HEADS, s, 2 * HEAD_PAD), bf16)],
        compiler_params=_cparams("arbitrary", "arbitrary"), name="mla_attention",
    )(qn, qr, kn, kr, v)


def _mm_res_kernel(a_ref, w_ref, x_ref, mod_ref, o_ref, *, kgate):
    y = _dot(a_ref[...], w_ref[...])
    o_ref[...] = x_ref[...] + mod_ref[0][kgate:kgate + 1, :] * y


def matmul_residual(a, w, x, mod, kgate, rows_per_mod, tm=512):
    t, k = a.shape
    return pl.pallas_call(
        functools.partial(_mm_res_kernel, kgate=kgate), grid=(t // tm,),
        in_specs=[pl.BlockSpec((tm, k), lambda i: (i, 0)), pl.BlockSpec((k, D), lambda i: (0, 0)),
                  pl.BlockSpec((tm, D), lambda i: (i, 0)),
                  pl.BlockSpec((1, N_MOD, D), lambda i: ((i * tm) // rows_per_mod, 0, 0))],
        out_specs=pl.BlockSpec((tm, D), lambda i: (i, 0)),
        out_shape=jax.ShapeDtypeStruct((t, D), f32),
        compiler_params=_cparams("arbitrary"), name="matmul_residual",
    )(a, w, x, mod)


INPROJ_BLOCK = 512
INPROJ_Z_BLOCKS = D_INNER // INPROJ_BLOCK
INPROJ_CONV_BLOCKS = CONV_DIM // INPROJ_BLOCK
DT_PAD = 128


def _inproj_kernel(x_ref, g_ref, mod_ref, w_ref, cw_ref, b_ref, z_ref, xbc_ref, dt_ref, h_ref, *, period):
    j = pl.program_id(1)

    @pl.when(j == 0)
    def _():
        h_ref[...] = _norm_mod(x_ref[...], g_ref[...], mod_ref[0], SHIFT1, SCALE1).astype(bf16)

    @pl.when(j < INPROJ_Z_BLOCKS)
    def _():
        z_ref[...] = _dot(h_ref[...], w_ref[...])

    @pl.when((j >= INPROJ_Z_BLOCKS) & (j < INPROJ_Z_BLOCKS + INPROJ_CONV_BLOCKS))
    def _():
        y = _dot(h_ref[...], w_ref[...])
        tm = y.shape[0]
        pos = lax.broadcasted_iota(i32, (tm, 1), 0) & (period - 1)
        prev = jnp.where(pos == 0, 0.0, pltpu.roll(y, 1, axis=0))
        nxt = jnp.where(pos == period - 1, 0.0, pltpu.roll(y, tm - 1, axis=0))
        cw = cw_ref[...]
        xbc_ref[...] = _silu(cw[0:1, :] * prev + cw[1:2, :] * y + cw[2:3, :] * nxt + b_ref[...])

    @pl.when(j == INPROJ_Z_BLOCKS + INPROJ_CONV_BLOCKS)
    def _():
        y = _dot(h_ref[...], w_ref[:, :DT_PAD]) + b_ref[:, :DT_PAD]
        dt_ref[...] = jnp.maximum(y, 0.0) + jnp.log1p(jnp.exp(-jnp.abs(y)))


def ssm_in_proj(x, g, mod, w_all, conv_w_all, bias_all, rows_per_mod, period, tm=2048):
    t = x.shape[0]
    nb = INPROJ_BLOCK
    nz, nx = INPROJ_Z_BLOCKS, INPROJ_CONV_BLOCKS
    col = lambda rows: pl.BlockSpec((rows, nb), lambda i, j: (0, j))
    return pl.pallas_call(
        functools.partial(_inproj_kernel, period=period), grid=(t // tm, nz + nx + 1),
        in_specs=[pl.BlockSpec((tm, D), lambda i, j: (i, 0)), pl.BlockSpec((1, D), lambda i, j: (0, 0)),
                  pl.BlockSpec((1, N_MOD, D), lambda i, j: ((i * tm) // rows_per_mod, 0, 0)),
                  col(D), col(3), col(1)],
        out_specs=[pl.BlockSpec((tm, nb), lambda i, j: (i, jnp.minimum(j, nz - 1))),
                   pl.BlockSpec((tm, nb), lambda i, j: (i, jnp.clip(j - nz, 0, nx - 1))),
                   pl.BlockSpec((tm, DT_PAD), lambda i, j: (i, 0))],
        out_shape=[jax.ShapeDtypeStruct((t, D_INNER), f32), jax.ShapeDtypeStruct((t, CONV_DIM), f32),
                   jax.ShapeDtypeStruct((t, DT_PAD), f32)],
        scratch_shapes=[pltpu.VMEM((tm, D), bf16)],
        compiler_params=_cparams("arbitrary", "arbitrary"), name="ssm_in_proj",
    )(x, g, mod, w_all, conv_w_all, bias_all)


def _ssd_kernel(*refs, nc, has_h0, has_hout, n_prev):
    xbc_ref, z_ref, dt_ref, dtT_ref, bT_ref, alr_ref, alc_ref, dsk_ref, ng_ref = refs[:9]
    rest = list(refs[9:])
    h0_ref = rest.pop(0) if has_h0 else None
    prev_ref = rest.pop(0) if n_prev else None
    y_ref = rest.pop(0)
    hout_ref = rest.pop(0) if has_hout else None
    state_all, yf_all, yc_all, col_all, row_all = rest
    q = CHUNK
    nb = xbc_ref.shape[0]
    d = pl.program_id(1)
    c = pl.program_id(2)
    ce = jnp.where(d == 0, c, nc - 1 - c)

    @pl.when(c == 0)
    def _():
        if has_h0:
            state_all[...] = h0_ref[:, 0]
        else:
            state_all[...] = jnp.zeros_like(state_all)

    ii = lax.broadcasted_iota(i32, (q, q), 0)
    jj = lax.broadcasted_iota(i32, (q, q), 1)
    ahead = (ii - jj) * jnp.where(d == 0, 1, -1)
    causal = ahead >= 0
    first_head = lax.broadcasted_iota(i32, (1, 2 * SSM_P), 1) < SSM_P
    m_col = jnp.where(causal, 1.0, 0.0).astype(f32)
    m_row = jnp.where(ahead <= 0, 1.0, 0.0).astype(f32)

    a_row = -jnp.exp(alr_ref[0])
    a_col = -jnp.exp(alc_ref[0])
    rows = pl.ds(pl.multiple_of(ce * q, q), q)
    for bi in range(nb):
        _ssd_decays(bi, m_col, m_row, a_row, a_col, dt_ref, dtT_ref, col_all.at[bi], row_all.at[bi])
    c_off = D_INNER + SSM_G * SSM_N
    for g in range(SSM_G):
        grp = []
        for bi in range(nb):
            b_g = xbc_ref[bi, :, D_INNER + g * SSM_N: D_INNER + (g + 1) * SSM_N].astype(bf16)
            c_g = xbc_ref[bi, :, c_off + g * SSM_N: c_off + (g + 1) * SSM_N].astype(bf16)
            grp.append((c_g, _dot_nt(c_g, b_g), bT_ref[bi, g * SSM_N:(g + 1) * SSM_N, :]))
        for pr in range(HEADS_PER_GROUP // 2):
            for bi in range(nb):
                _ssd_head_pair(bi, g * HEADS_PER_GROUP + 2 * pr, causal, first_head, *grp[bi], xbc_ref,
                               state_all.at[bi], yc_all.at[bi], col_all.at[bi], row_all.at[bi])

    @pl.when(d == 0)
    def _():
        yf_all[:, rows, :] = yc_all[...]

    @pl.when(d == 1)
    def _():
        for bi in range(nb):
            y = yf_all[bi, rows, :] + yc_all[bi] + dsk_ref[...] * xbc_ref[bi, :, :D_INNER]
            y = y * _silu(z_ref[bi])
            y_ref[bi] = _rms(y, ng_ref[...]).astype(bf16)

    if has_hout:
        @pl.when(c == nc - 1)
        def _():
            for bi in range(nb):
                if n_prev:
                    hout_ref[bi, :n_prev, 0] = prev_ref[bi, :, 0]
                hout_ref[bi, n_prev, 0] = state_all[bi].T


def _ssd_decays(bi, m_col, m_row, a_row, a_col, dt_ref, dtT_ref, col_ref, row_ref):
    q = CHUNK
    dt = dt_ref[0, bi]
    dtT = dtT_ref[0, bi]
    daT = dtT * a_col
    cs_col = _dot_f32(m_col, dt * a_row)
    cs_row = _dot_f32(daT, m_row)
    tot = jnp.sum(daT, axis=1, keepdims=True)
    col_ref[0] = cs_col
    col_ref[1] = jnp.exp(cs_col)
    row_ref[0] = cs_row
    row_ref[1] = dtT
    row_ref[2] = jnp.exp(tot - cs_row) * dtT
    row_ref[3] = jnp.broadcast_to(jnp.exp(tot), (SSM_H, q))


def _ssd_head_pair(bi, h0, causal, first_head, c_g, cb, bT_g, xbc_ref, state_ref, yc_ref, col_ref, row_ref):
    h1 = h0 + 1
    ps = slice(h0 * SSM_P, (h1 + 1) * SSM_P)
    x2 = xbc_ref[bi, :, ps]
    x_blk = jnp.concatenate([jnp.where(first_head, x2, 0.0), jnp.where(first_head, 0.0, x2)], axis=0).astype(bf16)
    st = state_ref[:, ps]
    ws, bs = [], []
    for h in (h0, h1):
        seg = col_ref[0, :, h:h + 1] - row_ref[0, h:h + 1, :]
        w = cb * jnp.exp(jnp.where(causal, seg, -jnp.inf)) * row_ref[1, h:h + 1, :]
        ws.append(w.astype(bf16))
        bs.append((bT_g * row_ref[2, h:h + 1, :]).astype(bf16))
    e_cs = jnp.where(first_head, col_ref[1, :, h0:h0 + 1], col_ref[1, :, h1:h1 + 1])
    yc_ref[:, ps] = _dot(jnp.concatenate(ws, axis=1), x_blk) + _dot(c_g, st.astype(bf16)) * e_cs
    dec = jnp.where(first_head, row_ref[3, h0:h0 + 1, :], row_ref[3, h1:h1 + 1, :])
    state_ref[:, ps] = st * dec + _dot(jnp.concatenate(bs, axis=1), x_blk)


def ssd_scan(xbc, z, dt_dir, dtT_dir, bT, a_log, d_skip_row, norm_g, h0, want_state=False, prev_states=None, nb=1):
    b, l, _ = xbc.shape
    nc = l // CHUNK
    q = CHUNK
    has_h0 = h0 is not None
    has_hout = want_state
    n_prev = 0 if prev_states is None else prev_states.shape[1]
    ce = lambda d, c: jnp.where(d == 0, c, nc - 1 - c)
    late = lambda d, c: jnp.where(d == 0, nc - 1, nc - 1 - c)
    in_specs = [
        pl.BlockSpec((nb, q, CONV_DIM), lambda i, d, c: (i, ce(d, c), 0)),
        pl.BlockSpec((nb, q, D_INNER), lambda i, d, c: (i, late(d, c), 0)),
        pl.BlockSpec((1, nb, q, SSM_H), lambda i, d, c: (d, i, ce(d, c), 0)),
        pl.BlockSpec((1, nb, SSM_H, q), lambda i, d, c: (d, i, 0, ce(d, c))),
        pl.BlockSpec((nb, SSM_G * SSM_N, q), lambda i, d, c: (i, 0, ce(d, c))),
        pl.BlockSpec((1, 1, SSM_H), lambda i, d, c: (d, 0, 0)),
        pl.BlockSpec((1, SSM_H, 1), lambda i, d, c: (d, 0, 0)),
        pl.BlockSpec((1, D_INNER), lambda i, d, c: (0, 0)),
        pl.BlockSpec((1, D_INNER), lambda i, d, c: (0, 0)),
    ]
    args = [xbc, z, dt_dir, dtT_dir, bT, a_log.reshape(2, 1, SSM_H), a_log.reshape(2, SSM_H, 1), d_skip_row, norm_g]
    if has_h0:
        in_specs.append(pl.BlockSpec((nb, 1, SSM_N, D_INNER), lambda i, d, c: (i, d, 0, 0)))
        args.append(h0)
    out_specs = [pl.BlockSpec((nb, q, D_INNER), lambda i, d, c: (i, late(d, c), 0))]
    out_shape = [jax.ShapeDtypeStruct((b, l, D_INNER), bf16)]
    if has_hout:
        if n_prev:
            in_specs.append(pl.BlockSpec((nb, n_prev, 1, D_INNER, SSM_N), lambda i, d, c: (i, 0, d, 0, 0)))
            args.append(prev_states)
        out_specs.append(pl.BlockSpec((nb, n_prev + 1, 1, D_INNER, SSM_N), lambda i, d, c: (i, 0, d, 0, 0)))
        out_shape.append(jax.ShapeDtypeStruct((b, n_prev + 1, 2, D_INNER, SSM_N), f32))
    return pl.pallas_call(
        functools.partial(_ssd_kernel, nc=nc, has_h0=has_h0, has_hout=has_hout, n_prev=n_prev),
        grid=(b // nb, 2, nc),
        in_specs=in_specs, out_specs=out_specs, out_shape=out_shape,
        scratch_shapes=[
            pltpu.VMEM((nb, SSM_N, D_INNER), f32),
            pltpu.VMEM((nb, l, D_INNER), f32),
            pltpu.VMEM((nb, q, D_INNER), f32),
            pltpu.VMEM((nb, 2, q, SSM_H), f32),
            pltpu.VMEM((nb, 4, SSM_H, q), f32),
        ],
        compiler_params=_cparams("arbitrary", "arbitrary", "arbitrary"), name="ssd_scan",
    )(*args)


def _router_kernel(x_ref, g_ref, mod_ref, wr_ref, h_ref, aff_ref):
    h = _norm_mod(x_ref[...], g_ref[...], mod_ref[0], SHIFT2, SCALE2)
    hb = h.astype(bf16)
    h_ref[...] = hb
    h_lo = (h - hb.astype(f32)).astype(bf16)
    w = wr_ref[...]
    w_hi = w.astype(bf16)
    w_lo = (w - w_hi.astype(f32)).astype(bf16)
    lg = _dot_nt(w_hi, hb) + _dot_nt(w_hi, h_lo) + _dot_nt(w_lo, hb)
    e = jnp.exp(lg - jnp.max(lg, axis=0, keepdims=True))
    aff_ref[...] = e / jnp.sum(e, axis=0, keepdims=True)


def moe_router(x, g, mod, w_router_t, rows_per_mod, tm=512):
    t = x.shape[0]
    return pl.pallas_call(
        _router_kernel, grid=(t // tm,),
        in_specs=[pl.BlockSpec((tm, D), lambda i: (i, 0)), pl.BlockSpec((1, D), lambda i: (0, 0)),
                  pl.BlockSpec((1, N_MOD, D), lambda i: ((i * tm) // rows_per_mod, 0, 0)),
                  pl.BlockSpec((N_EXPERTS, D), lambda i: (0, 0))],
        out_specs=[pl.BlockSpec((tm, D), lambda i: (i, 0)), pl.BlockSpec((N_EXPERTS, tm), lambda i: (0, i))],
        out_shape=[jax.ShapeDtypeStruct((t, D), bf16), jax.ShapeDtypeStruct((N_EXPERTS, t), f32)],
        compiler_params=_cparams("arbitrary"), name="moe_router",
    )(x, g, mod, w_router_t)


THRESHOLD_REFINE_STEPS = 28
META_HI = 8
META_NROUND = 64


def _route_kernel(aff_ref, key_ref, start_ref, meta_ref, *, cap, window):
    t = aff_ref.shape[1]
    aff = aff_ref[...]

    def count_ge(v):
        return jnp.sum(jnp.where(aff >= v, 1.0, 0.0), axis=1, keepdims=True)

    bits = lax.bitcast_convert_type(aff, i32)
    tb = jnp.zeros((N_EXPERTS, 1), i32)
    for bit in range(30, -1, -1):
        cand = tb | (1 << bit)
        cnt = jnp.sum((bits >= cand).astype(i32), axis=1, keepdims=True)
        tb = jnp.where(cnt >= cap, cand, tb)
    approx = lax.bitcast_convert_type(tb, f32)
    lo = jnp.where(count_ge(0.5 * approx) >= cap, 0.5 * approx, 0.0)
    hi = jnp.where(count_ge(2.0 * approx) < cap, 2.0 * approx, 2.0)

    def refine(_, lh):
        lo, hi = lh
        mid = 0.5 * (lo + hi)
        ge = count_ge(mid) >= cap
        return jnp.where(ge, mid, lo), jnp.where(ge, hi, mid)

    thr, _ = lax.fori_loop(0, THRESHOLD_REFINE_STEPS, refine, (lo, hi))
    gt = aff > thr
    eq = aff == thr
    gtf = jnp.where(gt, 1.0, 0.0)
    eqf = jnp.where(eq, 1.0, 0.0)
    need = cap - jnp.sum(gtf, axis=1, keepdims=True).astype(i32)
    blk = TOKEN_BLOCK
    tri = (lax.broadcasted_iota(i32, (blk, blk), 0) <= lax.broadcasted_iota(i32, (blk, blk), 1))
    tri = jnp.where(tri, 1.0, 0.0).astype(bf16)
    carry = jnp.zeros((2 * N_EXPERTS, 1), f32)
    start_ref[...] = jnp.zeros_like(start_ref)
    meta_ref[...] = jnp.zeros_like(meta_ref)
    before = jnp.zeros((N_EXPERTS, 1), i32)
    nsub = cap // GATHER_TILE
    assert nsub <= META_HI
    first_blk = [jnp.zeros((N_EXPERTS, 1), i32) for _ in range(nsub)]
    last_blk = [jnp.zeros((N_EXPERTS, 1), i32) for _ in range(nsub)]
    for j in range(t // blk):
        sl = slice(j * blk, (j + 1) * blk)
        m = jnp.concatenate([gtf[:, sl], eqf[:, sl]], axis=0)
        pc = _dot(m.astype(bf16), tri) + carry
        carry = pc[:, blk - 1:blk]
        cs_gt = pc[:N_EXPERTS].astype(i32)
        cs_eq = pc[N_EXPERTS:].astype(i32)
        sel = gt[:, sl] | (eq[:, sl] & (cs_eq <= need))
        cs = cs_gt + jnp.minimum(cs_eq, need)
        key_ref[:, sl] = jnp.where(sel, cs, 0)
        end = cs[:, blk - 1:blk]
        start = before & ~7
        start_ref[:, j:j + 1] = start
        rounds = (end - start + (window - 1)) // window
        meta_ref[:, META_NROUND + j:META_NROUND + j + 1] = jnp.broadcast_to(
            jnp.max(rounds, axis=0, keepdims=True), (N_EXPERTS, 1))
        for s in range(nsub):
            first_blk[s] = first_blk[s] + jnp.where(end <= s * GATHER_TILE, 1, 0)
            last_blk[s] = last_blk[s] + jnp.where(end < (s + 1) * GATHER_TILE, 1, 0)
        before = end
    for s in range(nsub):
        meta_ref[:, s:s + 1] = first_blk[s]
        meta_ref[:, META_HI + s:META_HI + s + 1] = last_blk[s]


def moe_route(aff_t, cap, window):
    t = aff_t.shape[1]
    small = jax.ShapeDtypeStruct((N_EXPERTS, 128), i32)
    small_spec = pl.BlockSpec((N_EXPERTS, 128), lambda i: (0, 0))
    return pl.pallas_call(
        functools.partial(_route_kernel, cap=cap, window=window), grid=(1,),
        in_specs=[pl.BlockSpec((N_EXPERTS, t), lambda i: (0, 0))],
        out_specs=[pl.BlockSpec((N_EXPERTS, t), lambda i: (0, 0)), small_spec, small_spec],
        out_shape=[jax.ShapeDtypeStruct((N_EXPERTS, t), i32), small, small],
        compiler_params=_cparams("arbitrary"), name="moe_route",
    )(aff_t)


def _moe_ffn_kernel(meta_ref, key_ref, aff_ref, h_ref, wg_ref, wu_ref, wd_ref, o_ref, acc_ref, gate_ref):
    e = pl.program_id(0)
    j = pl.program_id(1)
    tm = acc_ref.shape[0]
    tb = TOKEN_BLOCK
    acc_ref[...] = jnp.zeros_like(acc_ref)
    gate_ref[...] = jnp.zeros_like(gate_ref)
    gt = GATHER_TILE
    nsub = tm // gt
    nblk = key_ref.shape[1]
    gb = GATHER_BLOCKS
    row_id = lax.broadcasted_iota(i32, (gt, 1), 0)
    for s in range(nsub):
        sub = j * nsub + s
        srows = slice(s * gt, (s + 1) * gt)
        first = meta_ref[e, sub]
        last = meta_ref[e, META_HI + sub]

        def window(w, carry, sub=sub, srows=srows, first=first):
            begin = first + w * gb
            start = jnp.minimum(begin, nblk - gb)
            pieces = []
            gate = jnp.zeros((gt, 1), f32)
            for k in range(gb):
                b = start + k
                slot = jnp.where(b >= begin, sub * gt + 1, -gt) + row_id
                oh = key_ref[0, b] == slot
                pieces.append(jnp.where(oh, 1.0, 0.0).astype(bf16))
                gate = gate + jnp.sum(jnp.where(oh, aff_ref[0, b], 0.0), axis=1, keepdims=True)
            rows = pl.ds(pl.multiple_of(start * tb, tb), gb * tb)
            acc_ref[srows, :] += _dot(jnp.concatenate(pieces, axis=1), h_ref[rows, :])
            gate_ref[srows, :] += gate
            return carry

        lax.fori_loop(0, (last - first) // gb + 1, window, 0)
    xe = acc_ref[...].astype(bf16)
    hid = _silu(_dot(xe, wg_ref[0, 0].astype(bf16))) * _dot(xe, wu_ref[0, 0].astype(bf16))
    o_ref[0] = _dot(hid.astype(bf16), wd_ref[0, 0].astype(bf16)) * gate_ref[...]


def moe_ffn(meta, key, aff_t, h, wg, wu, wd, layer, cap):
    t = h.shape[0]
    tm = SLOT_TILE
    nblk = t // TOKEN_BLOCK
    nj = cap // tm
    key4 = key.reshape(N_EXPERTS, nblk, 1, TOKEN_BLOCK)
    aff4 = aff_t.reshape(N_EXPERTS, nblk, 1, TOKEN_BLOCK)
    row_spec = pl.BlockSpec((1, nblk, 1, TOKEN_BLOCK), lambda e, j, *_: (e, 0, 0, 0))
    wspec = pl.BlockSpec((1, 1, D, D_FF), lambda e, j, *_: (layer, e, 0, 0))
    grid_spec = pltpu.PrefetchScalarGridSpec(
        num_scalar_prefetch=1, grid=(N_EXPERTS, nj),
        in_specs=[row_spec, row_spec,
                  pl.BlockSpec((t, D), lambda e, j, *_: (0, 0), pipeline_mode=pl.Buffered(1)),
                  wspec, wspec, pl.BlockSpec((1, 1, D_FF, D), lambda e, j, *_: (layer, e, 0, 0))],
        out_specs=pl.BlockSpec((1, tm, D), lambda e, j, *_: (e, j, 0)),
        scratch_shapes=[pltpu.VMEM((tm, D), f32), pltpu.VMEM((tm, 1), f32)])
    return pl.pallas_call(
        _moe_ffn_kernel, grid_spec=grid_spec,
        out_shape=jax.ShapeDtypeStruct((N_EXPERTS, cap, D), f32),
        compiler_params=_cparams("arbitrary", "arbitrary"), name="moe_ffn",
    )(meta, key4, aff4, h, wg, wu, wd)


def _moe_combine_kernel(start_ref, meta_ref, key_ref, ye_hbm, x_ref, mod_ref, fg_ref, o_ref, stage_ref, acc_ref, sem,
                        *, cap, nblk, window, final_norm):
    i = pl.program_id(0)
    w = window

    def window_start(tile, e, r):
        return start_ref[e, tile] + r * w

    def copies(tile, r, buf):
        out = []
        for e in range(N_EXPERTS):
            first = pl.multiple_of(jnp.minimum(window_start(tile, e, r), cap - w), 8)
            out.append(pltpu.make_async_copy(ye_hbm.at[e, pl.ds(first, w), :],
                                             stage_ref.at[buf, pl.ds(e * w, w), :], sem.at[buf]))
        return out

    def accumulate(r, buf):
        pieces = []
        for e in range(N_EXPERTS):
            begin = window_start(i, e, r)
            slot = jnp.minimum(begin, cap - w) + lax.broadcasted_iota(i32, (w, 1), 0) + 1
            oh = (key_ref[e:e + 1, :] == slot) & (slot > begin)
            pieces.append(jnp.where(oh, 1.0, 0.0))
        oh = jnp.concatenate(pieces, axis=0).astype(bf16)
        rows = stage_ref[buf]
        hi = rows.astype(bf16)
        lo = (rows - hi.astype(f32)).astype(bf16)
        acc_ref[...] += _dot_tn(oh, hi) + _dot_tn(oh, lo)

    cur = lax.rem(i, 2)

    @pl.when(i == 0)
    def _():
        for cp in copies(0, 0, 0):
            cp.start()

    @pl.when(i + 1 < nblk)
    def _():
        for cp in copies(i + 1, 0, 1 - cur):
            cp.start()

    acc_ref[...] = jnp.zeros_like(acc_ref)
    for cp in copies(i, 0, cur):
        cp.wait()
    accumulate(0, cur)

    def extra_round(r, carry):
        cps = copies(i, r, 2)
        for cp in cps:
            cp.start()
        for cp in cps:
            cp.wait()
        accumulate(r, 2)
        return carry

    lax.fori_loop(1, meta_ref[0, META_NROUND + i], extra_round, 0)
    out = x_ref[...] + mod_ref[0][GATE2:GATE2 + 1, :] * acc_ref[...]
    if final_norm:
        out = _rms(out, fg_ref[...])
    o_ref[...] = out


def moe_combine(starts, meta, key, ye, x, mod, final_g, cap, rows_per_mod, window, final_norm):
    t = x.shape[0]
    tb = TOKEN_BLOCK
    nblk = t // tb
    grid_spec = pltpu.PrefetchScalarGridSpec(
        num_scalar_prefetch=2, grid=(nblk,),
        in_specs=[pl.BlockSpec((N_EXPERTS, tb), lambda i, *_: (0, i)),
                  pl.BlockSpec(memory_space=pl.ANY),
                  pl.BlockSpec((tb, D), lambda i, *_: (i, 0)),
                  pl.BlockSpec((1, N_MOD, D), lambda i, *_: ((i * tb) // rows_per_mod, 0, 0)),
                  pl.BlockSpec((1, D), lambda i, *_: (0, 0))],
        out_specs=pl.BlockSpec((tb, D), lambda i, *_: (i, 0)),
        scratch_shapes=[pltpu.VMEM((3, N_EXPERTS * window, D), f32), pltpu.VMEM((tb, D), f32),
                        pltpu.SemaphoreType.DMA((3,))])
    return pl.pallas_call(
        functools.partial(_moe_combine_kernel, cap=cap, nblk=nblk, window=window, final_norm=final_norm),
        grid_spec=grid_spec,
        out_shape=jax.ShapeDtypeStruct((t, D), f32),
        compiler_params=_cparams("arbitrary"), name="moe_combine",
    )(starts, meta, key, ye, x, mod, final_g)


def moe_layer(x, g2, mod, w_router_t, wg, wu, wd, final_g, layer, rows_per_mod, window):
    t = x.shape[0]
    cap = 2 * t // N_EXPERTS
    h, aff_t = moe_router(x, g2, mod, w_router_t, rows_per_mod)
    key, starts, meta = moe_route(aff_t, cap, window)
    ye = moe_ffn(meta, key, aff_t, h, wg, wu, wd, layer, cap)
    return moe_combine(starts, meta, key, ye, x, mod, final_g, cap, rows_per_mod, window, layer == DEPTH - 1)


def _rope_tables(seq):
    rows = seq // GRID_W
    row = jnp.repeat(jnp.arange(rows), GRID_W).astype(f32)
    col = jnp.tile(jnp.arange(GRID_W), rows).astype(f32)
    pairs = QK_ROPE // 4
    inv = ROPE_THETA ** (-jnp.arange(pairs, dtype=f32) / pairs)
    ang = jnp.concatenate([row[:, None] * inv, col[:, None] * inv], axis=-1)
    cos, sin = jnp.cos(ang), jnp.sin(ang)
    zero = jnp.zeros_like(cos)
    c = jnp.concatenate([cos, cos, zero, zero], axis=-1)
    s1 = jnp.concatenate([-sin, zero, zero, zero], axis=-1)
    s2 = jnp.concatenate([zero, sin, zero, zero], axis=-1)
    return c, s1, s2


def _mla_weights(w_dq, q_norm_g, w_uq, w_dkv, kv_norm_g, w_ukv, w_o):
    per_head = QK_NOPE + QK_ROPE
    uq = w_uq.reshape(Q_LORA, MLA_HEADS, per_head)
    uq_nope = uq[..., :QK_NOPE].reshape(Q_LORA, MLA_HEADS * QK_NOPE)
    uq_rope = jnp.pad(uq[..., QK_NOPE:], ((0, 0), (0, 0), (0, HEAD_PAD - QK_ROPE))).reshape(Q_LORA, MLA_HEADS * HEAD_PAD)
    wuq = jnp.concatenate([uq_nope, uq_rope], axis=1).astype(bf16)
    wdkv = jnp.pad(w_dkv, ((0, 0), (0, HEAD_PAD - QK_ROPE))).astype(bf16)
    ukv = w_ukv.reshape(KV_LORA, MLA_HEADS, QK_NOPE + V_HEAD)
    wukv = jnp.concatenate([ukv[..., :QK_NOPE].reshape(KV_LORA, -1), ukv[..., QK_NOPE:].reshape(KV_LORA, -1)], axis=1).astype(bf16)
    proj = (w_dq.astype(bf16), q_norm_g.reshape(1, Q_LORA), wuq, wdkv, kv_norm_g.reshape(1, KV_LORA), wukv)
    return proj, wukv, w_o.astype(bf16)


def _mla_layer(xp, xs, modp, mods, g1, cache_ckv_j, cache_krope_j, w, rope_tabs):
    proj_w, wukv, w_o = w
    bp, bs = xp.shape[0] // 256, xs.shape[0] // 2048
    nh = MLA_HEADS * HEAD_PAD
    qn, qr, kn, v, kr, ckv, krope = mla_project(xp, g1, modp, proj_w, None, xp.shape[0])
    r3 = lambda a, b: a.reshape(b, -1, a.shape[-1])
    op = mla_attention(r3(qn, bp), r3(qr, bp), r3(kn, bp), r3(kr, bp), r3(v, bp), nb=2).reshape(-1, nh)
    xp = matmul_residual(op, w_o, xp, modp, GATE1, xp.shape[0])
    new_ckv = ckv.reshape(bp, -1, KV_LORA)
    new_krope = krope[:, :QK_ROPE].reshape(bp, -1, QK_ROPE)
    qn, qr, kn, v, kr, _, _ = mla_project(xs, g1, mods, proj_w, rope_tabs, 2048)
    ctx = matmul_bf16(cache_ckv_j.reshape(-1, KV_LORA).astype(bf16), wukv, tm=512)
    nk = MLA_HEADS * QK_NOPE
    kn_all = jnp.concatenate([ctx[:, :nk].reshape(bs, -1, nk), r3(kn, bs)], axis=1)
    v_all = jnp.concatenate([ctx[:, nk:].reshape(bs, -1, nk), r3(v, bs)], axis=1)
    kr_ctx = jnp.pad(cache_krope_j, ((0, 0), (0, 0), (0, HEAD_PAD - QK_ROPE))).astype(bf16)
    kr_all = jnp.concatenate([kr_ctx, r3(kr, bs)], axis=1)
    os_ = mla_attention(r3(qn, bs), r3(qr, bs), kn_all, kr_all, v_all).reshape(-1, nh)
    xs = matmul_residual(os_, w_o, xs, mods, GATE1, 2048)
    return xp, xs, new_ckv, new_krope


def _ssm_stream(x, mod, g1, w, rows_per_mod, seq, h0, want_state=False, prev_states=None):
    w_all, conv_w_all, bias_all, a_log, d_skip_row, norm_g, w_out = w
    b = x.shape[0] // seq
    z, xbc, dt = ssm_in_proj(x, g1, mod, w_all, conv_w_all, bias_all, rows_per_mod, seq)
    dt4 = dt[:, :2 * SSM_H].reshape(b, seq, 2, SSM_H)
    dt_dir = dt4.transpose(2, 0, 1, 3)
    dtT_dir = dt4.transpose(2, 0, 3, 1)
    xbc3 = xbc.reshape(b, seq, CONV_DIM)
    bT = xbc3[:, :, D_INNER:D_INNER + SSM_G * SSM_N].transpose(0, 2, 1)
    outs = ssd_scan(xbc3, z.reshape(b, seq, D_INNER), dt_dir, dtT_dir, bT, a_log, d_skip_row, norm_g, h0,
                    want_state, prev_states, nb=2 if seq <= 2 * CHUNK else 1)
    x = matmul_residual(outs[0].reshape(-1, D_INNER), w_out, x, mod, GATE1, rows_per_mod)
    return x, (outs[1] if want_state else None)


def kernel(x_prompt, x_sample, cache_ckv, cache_krope, state_ssm, c, c_ctx, w_mod, b_mod, norm1_g, norm2_g, final_norm_g, mla_w_dq, mla_q_norm_g, mla_w_uq, mla_w_dkv, mla_kv_norm_g, mla_w_ukv, mla_w_o, ssm_w_in, ssm_conv_w, ssm_conv_b, ssm_dt_bias, ssm_a_log, ssm_d_skip, ssm_norm_g, ssm_w_out, moe_w_router, moe_w_gate, moe_w_up, moe_w_down):
    bp, lp, _ = x_prompt.shape
    bs, ls, _ = x_sample.shape
    xp = x_prompt.reshape(bp * lp, D)
    xs = x_sample.reshape(bs * ls, D)

    cond8 = jnp.concatenate([c_ctx[None, :], c, jnp.zeros((8 - 1 - bs, D), f32)], axis=0)
    mod_all = modulation_all(cond8, w_mod, b_mod)
    rope_tabs = _rope_tables(ls)

    new_ckv, new_krope = [], []
    new_ssm = None
    for l in range(DEPTH):
        modp = mod_all[l, 0:1].reshape(1, N_MOD, D)
        mods = mod_all[l, 1:1 + bs].reshape(bs, N_MOD, D)
        g1 = norm1_g[l].reshape(1, D)
        j = l // 2
        if l % 2 == 0:
            w = _mla_weights(mla_w_dq[j], mla_q_norm_g[j], mla_w_uq[j], mla_w_dkv[j], mla_kv_norm_g[j], mla_w_ukv[j], mla_w_o[j])
            xp, xs, ckv, krope = _mla_layer(xp, xs, modp, mods, g1, cache_ckv[:, j], cache_krope[:, j], w, rope_tabs)
            new_ckv.append(ckv)
            new_krope.append(krope)
        else:
            dt_cols = INPROJ_BLOCK - 2 * SSM_H
            w_all = jnp.pad(ssm_w_in[j], ((0, 0), (0, dt_cols))).astype(bf16)
            conv_w_all = jnp.pad(ssm_conv_w[j].T, ((0, 0), (D_INNER, INPROJ_BLOCK)))
            bias_all = jnp.concatenate([jnp.zeros((D_INNER,), f32), ssm_conv_b[j], ssm_dt_bias[j].reshape(-1),
                                        jnp.zeros((dt_cols,), f32)]).reshape(1, -1)
            w = (w_all, conv_w_all, bias_all, ssm_a_log[j],
                 jnp.repeat(ssm_d_skip[j], SSM_P).reshape(1, D_INNER), ssm_norm_g[j].reshape(1, D_INNER),
                 ssm_w_out[j].astype(bf16))
            xp, new_ssm = _ssm_stream(xp, modp, g1, w, bp * lp, lp, None, want_state=True, prev_states=new_ssm)
            h0 = state_ssm[:, j].transpose(0, 1, 4, 2, 3).reshape(bs, 2, SSM_N, D_INNER)
            xs, _ = _ssm_stream(xs, mods, g1, w, ls, ls, h0)
        g2 = norm2_g[l].reshape(1, D)
        moe_w = (moe_w_router[l].T, moe_w_gate, moe_w_up, moe_w_down, final_norm_g.reshape(1, D), l)
        xp = moe_layer(xp, g2, modp, *moe_w, bp * lp, COMBINE_WINDOW_PROMPT)
        xs = moe_layer(xs, g2, mods, *moe_w, ls, COMBINE_WINDOW_SAMPLE)

    y_prompt = xp.reshape(bp, lp, D)
    y_sample = xs.reshape(bs, ls, D)
    new_ssm = new_ssm.reshape(bp, DEPTH // 2, 2, SSM_H, SSM_P, SSM_N)
    return (y_prompt, y_sample, jnp.stack(new_ckv, axis=1), jnp.stack(new_krope, axis=1), new_ssm)
```

```python
import functools

import jax
import jax.numpy as jnp
from jax import lax
from jax.experimental import pallas as pl
from jax.experimental.pallas import tpu as pltpu

f32 = jnp.float32
bf16 = jnp.bfloat16
i32 = jnp.int32

D = 1024
DEPTH = 4
N_MOD = 6
EPS = 1e-6
GRID_W = 64

MLA_HEADS = 8
QK_NOPE = 128
QK_ROPE = 64
V_HEAD = 128
Q_LORA = 384
KV_LORA = 256
ROPE_THETA = 10000.0
ATTN_SCALE = (QK_NOPE + QK_ROPE) ** -0.5
HEAD_PAD = 128

D_INNER = 2 * D
SSM_P = 64
SSM_H = D_INNER // SSM_P
SSM_G = 4
SSM_N = 128
CHUNK = 128
CONV_DIM = D_INNER + 2 * SSM_G * SSM_N
HEADS_PER_GROUP = SSM_H // SSM_G
GROUP_W = HEADS_PER_GROUP * SSM_P

N_EXPERTS = 16
D_FF = 1024
TOKEN_BLOCK = 256
SLOT_TILE = 512
GATHER_TILE = 128
GATHER_BLOCKS_PROMPT = 4
GATHER_BLOCKS_SAMPLE = 8
COMBINE_WINDOW_PROMPT = 64
COMBINE_WINDOW_SAMPLE = 128

VMEM_LIMIT = 56 * 1024 * 1024

SHIFT1, SCALE1, GATE1, SHIFT2, SCALE2, GATE2 = range(6)


def _cparams(*sem):
    return pltpu.CompilerParams(dimension_semantics=sem, vmem_limit_bytes=VMEM_LIMIT)


def _dot(a, b):
    return jnp.dot(a, b, preferred_element_type=f32)


def _dot_nt(a, b):
    return lax.dot_general(a, b, (((1,), (1,)), ((), ())), preferred_element_type=f32)


def _dot_tn(a, b):
    return lax.dot_general(a, b, (((0,), (0,)), ((), ())), preferred_element_type=f32)


def _dot_f32(a, b):
    return jnp.dot(a, b, preferred_element_type=f32, precision=lax.Precision.HIGHEST)


def _rms(x, g):
    ms = jnp.mean(x * x, axis=-1, keepdims=True)
    return x * lax.rsqrt(ms + EPS) * g


def _norm_mod(x, g, mod, k_shift, k_scale):
    return _rms(x, g) * (1.0 + mod[k_scale:k_scale + 1, :]) + mod[k_shift:k_shift + 1, :]


def _silu(x):
    return x * jax.nn.sigmoid(x)


def _mod_kernel(c_ref, w_ref, b_ref, o_ref):
    s = _silu(c_ref[...]).astype(bf16)
    o_ref[0] = _dot(s, w_ref[0].astype(bf16)) + b_ref[0]


def modulation_all(cond8, w_mod, b_mod):
    nb = 1536
    n = N_MOD * D
    return pl.pallas_call(
        _mod_kernel,
        grid=(DEPTH, n // nb),
        in_specs=[
            pl.BlockSpec((8, D), lambda l, j: (0, 0)),
            pl.BlockSpec((1, D, nb), lambda l, j: (l, 0, j)),
            pl.BlockSpec((1, 1, nb), lambda l, j: (l, 0, j)),
        ],
        out_specs=pl.BlockSpec((1, 8, nb), lambda l, j: (l, 0, j)),
        out_shape=jax.ShapeDtypeStruct((DEPTH, 8, n), f32),
        compiler_params=_cparams("arbitrary", "arbitrary"),
        name="modulation",
    )(cond8, w_mod, b_mod.reshape(DEPTH, 1, n))


def _rope_rot(p, c, s1, s2):
    return p * c + pltpu.roll(p, 96, axis=1) * s1 + pltpu.roll(p, 32, axis=1) * s2


def _mla_proj_kernel(*refs, rope):
    if rope:
        (x_ref, g_ref, mod_ref, wdq_ref, qg_ref, wuq_ref, wdkv_ref, kvg_ref, wukv_ref, rc_ref, rs1_ref, rs2_ref,
         qn_ref, qr_ref, kn_ref, v_ref, kr_ref, ckv_ref, krope_ref) = refs
    else:
        (x_ref, g_ref, mod_ref, wdq_ref, qg_ref, wuq_ref, wdkv_ref, kvg_ref, wukv_ref,
         qn_ref, qr_ref, kn_ref, v_ref, kr_ref, ckv_ref, krope_ref) = refs
    h = _norm_mod(x_ref[...], g_ref[...], mod_ref[0], SHIFT1, SCALE1).astype(bf16)
    nq = MLA_HEADS * QK_NOPE
    q_lat = _rms(_dot(h, wdq_ref[...]), qg_ref[...]).astype(bf16)
    q = _dot(q_lat, wuq_ref[...])
    qn_ref[...] = q[:, :nq].astype(bf16)
    if rope:
        c, s1, s2 = rc_ref[...], rs1_ref[...], rs2_ref[...]
    for hh in range(MLA_HEADS):
        piece = q[:, nq + hh * HEAD_PAD: nq + (hh + 1) * HEAD_PAD]
        if rope:
            piece = _rope_rot(piece, c, s1, s2)
        qr_ref[:, hh * HEAD_PAD:(hh + 1) * HEAD_PAD] = piece.astype(bf16)
    kv = _dot(h, wdkv_ref[...])
    ckv = _rms(kv[:, :KV_LORA], kvg_ref[...])
    ckv_ref[...] = ckv
    kr = kv[:, KV_LORA:]
    krope_ref[...] = kr
    if rope:
        kr = _rope_rot(kr, c, s1, s2)
    kr_ref[...] = kr.astype(bf16)
    kvx = _dot(ckv.astype(bf16), wukv_ref[...])
    nk = MLA_HEADS * QK_NOPE
    kn_ref[...] = kvx[:, :nk].astype(bf16)
    v_ref[...] = kvx[:, nk:].astype(bf16)


def mla_project(x, g, mod, w, rope_tabs, rows_per_mod, tm=256):
    t = x.shape[0]
    wdq, qg, wuq, wdkv, kvg, wukv = w
    full = lambda a: pl.BlockSpec(a.shape, lambda i: (0,) * a.ndim)
    row = lambda n: pl.BlockSpec((tm, n), lambda i: (i, 0))
    in_specs = [row(D), full(g), pl.BlockSpec((1, N_MOD, D), lambda i: ((i * tm) // rows_per_mod, 0, 0)),
                full(wdq), full(qg), full(wuq), full(wdkv), full(kvg), full(wukv)]
    args = [x, g, mod, wdq, qg, wuq, wdkv, kvg, wukv]
    rope = rope_tabs is not None
    if rope:
        nrb = rope_tabs[0].shape[0] // tm
        in_specs += [pl.BlockSpec((tm, HEAD_PAD), lambda i: (i % nrb, 0))] * 3
        args += list(rope_tabs)
    nh = MLA_HEADS * HEAD_PAD
    out_shape = [jax.ShapeDtypeStruct((t, nh), bf16)] * 4 + [
        jax.ShapeDtypeStruct((t, HEAD_PAD), bf16), jax.ShapeDtypeStruct((t, KV_LORA), f32),
        jax.ShapeDtypeStruct((t, HEAD_PAD), f32)]
    out_specs = [row(nh)] * 4 + [row(HEAD_PAD), row(KV_LORA), row(HEAD_PAD)]
    return pl.pallas_call(
        functools.partial(_mla_proj_kernel, rope=rope),
        grid=(t // tm,), in_specs=in_specs, out_specs=out_specs, out_shape=out_shape,
        compiler_params=_cparams("arbitrary"), name="mla_project",
    )(*args)


def _mm_kernel(a_ref, w_ref, o_ref):
    o_ref[...] = _dot(a_ref[...], w_ref[...]).astype(o_ref.dtype)


def matmul_bf16(a, w, tm):
    m, k = a.shape
    n = w.shape[1]
    return pl.pallas_call(
        _mm_kernel, grid=(m // tm,),
        in_specs=[pl.BlockSpec((tm, k), lambda i: (i, 0)), pl.BlockSpec((k, n), lambda i: (0, 0))],
        out_specs=pl.BlockSpec((tm, n), lambda i: (i, 0)),
        out_shape=jax.ShapeDtypeStruct((m, n), bf16),
        compiler_params=_cparams("arbitrary"), name="matmul_bf16",
    )(a, w)


def _attn_kernel(*refs, nseg):
    qn_ref, qr_ref = refs[:2]
    segs = [refs[2 + 3 * i: 5 + 3 * i] for i in range(nseg)]
    o_ref, kcat_ref = refs[2 + 3 * nseg:]
    nb = qn_ref.shape[0]
    offs = [0]
    for kn_ref, _, _ in segs:
        offs.append(offs[-1] + kn_ref.shape[1])

    @pl.when(pl.program_id(1) == 0)
    def _():
        for b in range(nb):
            for (kn_ref, kr_ref, _), off in zip(segs, offs):
                rows = slice(off, off + kn_ref.shape[1])
                for h in range(MLA_HEADS):
                    kcat_ref[b, h, rows, :HEAD_PAD] = kn_ref[b, :, h * HEAD_PAD:(h + 1) * HEAD_PAD]
                    kcat_ref[b, h, rows, HEAD_PAD:] = kr_ref[b]

    for b in range(nb):
        for h in range(MLA_HEADS):
            sl = slice(h * HEAD_PAD, (h + 1) * HEAD_PAD)
            q = jnp.concatenate([qn_ref[b, :, sl], qr_ref[b, :, sl]], axis=1)
            s = _dot_nt(q, kcat_ref[b, h]) * ATTN_SCALE
            e = jnp.exp(s - jnp.max(s, axis=-1, keepdims=True))
            l = jnp.sum(e, axis=-1, keepdims=True)
            p = e.astype(bf16)
            o = sum(_dot(p[:, off:off + v_ref.shape[1]], v_ref[b, :, sl]) for (_, _, v_ref), off in zip(segs, offs))
            o_ref[b, :, sl] = (o / l).astype(bf16)


def mla_attention(qn, qr, kv_segments, tq=256, nb=1):
    b, lq, nh = qn.shape
    qspec = pl.BlockSpec((nb, tq, nh), lambda i, j: (i, j, 0))
    in_specs, args = [qspec, qspec], [qn, qr]
    for kn, kr, v in kv_segments:
        s = kn.shape[1]
        kspec = pl.BlockSpec((nb, s, nh), lambda i, j: (i, 0, 0))
        in_specs += [kspec, pl.BlockSpec((nb, s, HEAD_PAD), lambda i, j: (i, 0, 0)), kspec]
        args += [kn, kr, v]
    s_all = sum(kn.shape[1] for kn, _, _ in kv_segments)
    return pl.pallas_call(
        functools.partial(_attn_kernel, nseg=len(kv_segments)), grid=(b // nb, lq // tq),
        in_specs=in_specs, out_specs=qspec, out_shape=jax.ShapeDtypeStruct((b, lq, nh), bf16),
        scratch_shapes=[pltpu.VMEM((nb, MLA_HEADS, s_all, 2 * HEAD_PAD), bf16)],
        compiler_params=_cparams("arbitrary", "arbitrary"), name="mla_attention",
    )(*args)


def _mm_res_kernel(a_ref, w_ref, x_ref, mod_ref, o_ref, *, kgate):
    y = _dot(a_ref[...], w_ref[...])
    o_ref[...] = x_ref[...] + mod_ref[0][kgate:kgate + 1, :] * y


def matmul_residual(a, w, x, mod, kgate, rows_per_mod, tm=512):
    t, k = a.shape
    return pl.pallas_call(
        functools.partial(_mm_res_kernel, kgate=kgate), grid=(t // tm,),
        in_specs=[pl.BlockSpec((tm, k), lambda i: (i, 0)), pl.BlockSpec((k, D), lambda i: (0, 0)),
                  pl.BlockSpec((tm, D), lambda i: (i, 0)),
                  pl.BlockSpec((1, N_MOD, D), lambda i: ((i * tm) // rows_per_mod, 0, 0))],
        out_specs=pl.BlockSpec((tm, D), lambda i: (i, 0)),
        out_shape=jax.ShapeDtypeStruct((t, D), f32),
        compiler_params=_cparams("arbitrary"), name="matmul_residual",
    )(a, w, x, mod)


INPROJ_BLOCK = 512
INPROJ_Z_BLOCKS = D_INNER // INPROJ_BLOCK
INPROJ_CONV_BLOCKS = CONV_DIM // INPROJ_BLOCK
DT_PAD = 128


def _inproj_kernel(x_ref, g_ref, mod_ref, w_ref, cw_ref, b_ref, z_ref, xbc_ref, dt_ref, h_ref, *, period):
    j = pl.program_id(1)

    @pl.when(j == 0)
    def _():
        h_ref[...] = _norm_mod(x_ref[...], g_ref[...], mod_ref[0], SHIFT1, SCALE1).astype(bf16)

    @pl.when(j < INPROJ_Z_BLOCKS)
    def _():
        z_ref[...] = _dot(h_ref[...], w_ref[...])

    @pl.when((j >= INPROJ_Z_BLOCKS) & (j < INPROJ_Z_BLOCKS + INPROJ_CONV_BLOCKS))
    def _():
        y = _dot(h_ref[...], w_ref[...])
        tm = y.shape[0]
        pos = lax.broadcasted_iota(i32, (tm, 1), 0) & (period - 1)
        prev = jnp.where(pos == 0, 0.0, pltpu.roll(y, 1, axis=0))
        nxt = jnp.where(pos == period - 1, 0.0, pltpu.roll(y, tm - 1, axis=0))
        cw = cw_ref[...]
        xbc_ref[...] = _silu(cw[0:1, :] * prev + cw[1:2, :] * y + cw[2:3, :] * nxt + b_ref[...])

    @pl.when(j == INPROJ_Z_BLOCKS + INPROJ_CONV_BLOCKS)
    def _():
        y = _dot(h_ref[...], w_ref[:, :DT_PAD]) + b_ref[:, :DT_PAD]
        dt_ref[...] = jnp.maximum(y, 0.0) + jnp.log1p(jnp.exp(-jnp.abs(y)))


def ssm_in_proj(x, g, mod, w_all, conv_w_all, bias_all, rows_per_mod, period, tm=2048):
    t = x.shape[0]
    nb = INPROJ_BLOCK
    nz, nx = INPROJ_Z_BLOCKS, INPROJ_CONV_BLOCKS
    col = lambda rows: pl.BlockSpec((rows, nb), lambda i, j: (0, j))
    return pl.pallas_call(
        functools.partial(_inproj_kernel, period=period), grid=(t // tm, nz + nx + 1),
        in_specs=[pl.BlockSpec((tm, D), lambda i, j: (i, 0)), pl.BlockSpec((1, D), lambda i, j: (0, 0)),
                  pl.BlockSpec((1, N_MOD, D), lambda i, j: ((i * tm) // rows_per_mod, 0, 0)),
                  col(D), col(3), col(1)],
        out_specs=[pl.BlockSpec((tm, nb), lambda i, j: (i, jnp.minimum(j, nz - 1))),
                   pl.BlockSpec((tm, nb), lambda i, j: (i, jnp.clip(j - nz, 0, nx - 1))),
                   pl.BlockSpec((tm, DT_PAD), lambda i, j: (i, 0))],
        out_shape=[jax.ShapeDtypeStruct((t, D_INNER), f32), jax.ShapeDtypeStruct((t, CONV_DIM), f32),
                   jax.ShapeDtypeStruct((t, DT_PAD), f32)],
        scratch_shapes=[pltpu.VMEM((tm, D), bf16)],
        compiler_params=_cparams("arbitrary", "arbitrary"), name="ssm_in_proj",
    )(x, g, mod, w_all, conv_w_all, bias_all)


def _ssd_kernel(*refs, nc, has_h0, has_hout, n_prev):
    xbc_ref, z_ref, dt_ref, dtT_ref, bT_ref, alr_ref, alc_ref, dsk_ref, ng_ref = refs[:9]
    rest = list(refs[9:])
    h0_ref = rest.pop(0) if has_h0 else None
    prev_ref = rest.pop(0) if n_prev else None
    y_ref = rest.pop(0)
    hout_ref = rest.pop(0) if has_hout else None
    state_all, yf_all, yc_all, col_all, row_all = rest
    q = CHUNK
    nb = xbc_ref.shape[0]
    d = pl.program_id(1)
    c = pl.program_id(2)
    ce = jnp.where(d == 0, c, nc - 1 - c)

    @pl.when(c == 0)
    def _():
        if has_h0:
            state_all[...] = h0_ref[:, 0]
        else:
            state_all[...] = jnp.zeros_like(state_all)

    ii = lax.broadcasted_iota(i32, (q, q), 0)
    jj = lax.broadcasted_iota(i32, (q, q), 1)
    ahead = (ii - jj) * jnp.where(d == 0, 1, -1)
    causal = ahead >= 0
    first_head = lax.broadcasted_iota(i32, (1, 2 * SSM_P), 1) < SSM_P
    m_col = jnp.where(causal, 1.0, 0.0).astype(f32)
    m_row = jnp.where(ahead <= 0, 1.0, 0.0).astype(f32)

    a_row = -jnp.exp(alr_ref[0])
    a_col = -jnp.exp(alc_ref[0])
    rows = pl.ds(pl.multiple_of(ce * q, q), q)
    for bi in range(nb):
        _ssd_decays(bi, m_col, m_row, a_row, a_col, dt_ref, dtT_ref, col_all.at[bi], row_all.at[bi])
    c_off = D_INNER + SSM_G * SSM_N
    for g in range(SSM_G):
        grp = []
        for bi in range(nb):
            b_g = xbc_ref[bi, :, D_INNER + g * SSM_N: D_INNER + (g + 1) * SSM_N].astype(bf16)
            c_g = xbc_ref[bi, :, c_off + g * SSM_N: c_off + (g + 1) * SSM_N].astype(bf16)
            grp.append((c_g, _dot_nt(c_g, b_g), bT_ref[bi, g * SSM_N:(g + 1) * SSM_N, :]))
        for pr in range(HEADS_PER_GROUP // 2):
            for bi in range(nb):
                _ssd_head_pair(bi, g * HEADS_PER_GROUP + 2 * pr, causal, first_head, *grp[bi], xbc_ref,
                               state_all.at[bi], yc_all.at[bi], col_all.at[bi], row_all.at[bi])

    @pl.when(d == 0)
    def _():
        yf_all[:, rows, :] = yc_all[...]

    @pl.when(d == 1)
    def _():
        for bi in range(nb):
            y = yf_all[bi, rows, :] + yc_all[bi] + dsk_ref[...] * xbc_ref[bi, :, :D_INNER]
            y = y * _silu(z_ref[bi])
            y_ref[bi] = _rms(y, ng_ref[...]).astype(bf16)

    if has_hout:
        @pl.when(c == nc - 1)
        def _():
            for bi in range(nb):
                if n_prev:
                    hout_ref[bi, :n_prev, 0] = prev_ref[bi, :, 0]
                hout_ref[bi, n_prev, 0] = state_all[bi].T


def _ssd_decays(bi, m_col, m_row, a_row, a_col, dt_ref, dtT_ref, col_ref, row_ref):
    q = CHUNK
    dt = dt_ref[0, bi]
    dtT = dtT_ref[0, bi]
    daT = dtT * a_col
    cs_col = _dot_f32(m_col, dt * a_row)
    cs_row = _dot_f32(daT, m_row)
    tot = jnp.sum(daT, axis=1, keepdims=True)
    col_ref[0] = cs_col
    col_ref[1] = jnp.exp(cs_col)
    row_ref[0] = cs_row
    row_ref[1] = dtT
    row_ref[2] = jnp.exp(tot - cs_row) * dtT
    row_ref[3] = jnp.broadcast_to(jnp.exp(tot), (SSM_H, q))


def _ssd_head_pair(bi, h0, causal, first_head, c_g, cb, bT_g, xbc_ref, state_ref, yc_ref, col_ref, row_ref):
    h1 = h0 + 1
    ps = slice(h0 * SSM_P, (h1 + 1) * SSM_P)
    x2 = xbc_ref[bi, :, ps]
    x_blk = jnp.concatenate([jnp.where(first_head, x2, 0.0), jnp.where(first_head, 0.0, x2)], axis=0).astype(bf16)
    st = state_ref[:, ps]
    ws, bs = [], []
    for h in (h0, h1):
        seg = col_ref[0, :, h:h + 1] - row_ref[0, h:h + 1, :]
        w = cb * jnp.exp(jnp.where(causal, seg, -jnp.inf)) * row_ref[1, h:h + 1, :]
        ws.append(w.astype(bf16))
        bs.append((bT_g * row_ref[2, h:h + 1, :]).astype(bf16))
    e_cs = jnp.where(first_head, col_ref[1, :, h0:h0 + 1], col_ref[1, :, h1:h1 + 1])
    yc_ref[:, ps] = _dot(jnp.concatenate(ws, axis=1), x_blk) + _dot(c_g, st.astype(bf16)) * e_cs
    dec = jnp.where(first_head, row_ref[3, h0:h0 + 1, :], row_ref[3, h1:h1 + 1, :])
    state_ref[:, ps] = st * dec + _dot(jnp.concatenate(bs, axis=1), x_blk)


def ssd_scan(xbc, z, dt_dir, dtT_dir, bT, a_log, d_skip_row, norm_g, h0, want_state=False, prev_states=None, nb=1):
    b, l, _ = xbc.shape
    nc = l // CHUNK
    q = CHUNK
    has_h0 = h0 is not None
    has_hout = want_state
    n_prev = 0 if prev_states is None else prev_states.shape[1]
    ce = lambda d, c: jnp.where(d == 0, c, nc - 1 - c)
    late = lambda d, c: jnp.where(d == 0, nc - 1, nc - 1 - c)
    in_specs = [
        pl.BlockSpec((nb, q, CONV_DIM), lambda i, d, c: (i, ce(d, c), 0)),
        pl.BlockSpec((nb, q, D_INNER), lambda i, d, c: (i, late(d, c), 0)),
        pl.BlockSpec((1, nb, q, SSM_H), lambda i, d, c: (d, i, ce(d, c), 0)),
        pl.BlockSpec((1, nb, SSM_H, q), lambda i, d, c: (d, i, 0, ce(d, c))),
        pl.BlockSpec((nb, SSM_G * SSM_N, q), lambda i, d, c: (i, 0, ce(d, c))),
        pl.BlockSpec((1, 1, SSM_H), lambda i, d, c: (d, 0, 0)),
        pl.BlockSpec((1, SSM_H, 1), lambda i, d, c: (d, 0, 0)),
        pl.BlockSpec((1, D_INNER), lambda i, d, c: (0, 0)),
        pl.BlockSpec((1, D_INNER), lambda i, d, c: (0, 0)),
    ]
    args = [xbc, z, dt_dir, dtT_dir, bT, a_log.reshape(2, 1, SSM_H), a_log.reshape(2, SSM_H, 1), d_skip_row, norm_g]
    if has_h0:
        in_specs.append(pl.BlockSpec((nb, 1, SSM_N, D_INNER), lambda i, d, c: (i, d, 0, 0)))
        args.append(h0)
    out_specs = [pl.BlockSpec((nb, q, D_INNER), lambda i, d, c: (i, late(d, c), 0))]
    out_shape = [jax.ShapeDtypeStruct((b, l, D_INNER), bf16)]
    if has_hout:
        if n_prev:
            in_specs.append(pl.BlockSpec((nb, n_prev, 1, D_INNER, SSM_N), lambda i, d, c: (i, 0, d, 0, 0)))
            args.append(prev_states)
        out_specs.append(pl.BlockSpec((nb, n_prev + 1, 1, D_INNER, SSM_N), lambda i, d, c: (i, 0, d, 0, 0)))
        out_shape.append(jax.ShapeDtypeStruct((b, n_prev + 1, 2, D_INNER, SSM_N), f32))
    return pl.pallas_call(
        functools.partial(_ssd_kernel, nc=nc, has_h0=has_h0, has_hout=has_hout, n_prev=n_prev),
        grid=(b // nb, 2, nc),
        in_specs=in_specs, out_specs=out_specs, out_shape=out_shape,
        scratch_shapes=[
            pltpu.VMEM((nb, SSM_N, D_INNER), f32),
            pltpu.VMEM((nb, l, D_INNER), f32),
            pltpu.VMEM((nb, q, D_INNER), f32),
            pltpu.VMEM((nb, 2, q, SSM_H), f32),
            pltpu.VMEM((nb, 4, SSM_H, q), f32),
        ],
        compiler_params=_cparams("arbitrary", "arbitrary", "arbitrary"), name="ssd_scan",
    )(*args)


def _router_kernel(x_ref, g_ref, mod_ref, wr_ref, h_ref, aff_ref):
    h = _norm_mod(x_ref[...], g_ref[...], mod_ref[0], SHIFT2, SCALE2)
    hb = h.astype(bf16)
    h_ref[...] = hb
    h_lo = (h - hb.astype(f32)).astype(bf16)
    w = wr_ref[...]
    w_hi = w.astype(bf16)
    w_lo = (w - w_hi.astype(f32)).astype(bf16)
    lg = _dot_nt(w_hi, hb) + _dot_nt(w_hi, h_lo) + _dot_nt(w_lo, hb)
    e = jnp.exp(lg - jnp.max(lg, axis=0, keepdims=True))
    aff_ref[...] = e / jnp.sum(e, axis=0, keepdims=True)


def moe_router(x, g, mod, w_router_t, rows_per_mod, tm=512):
    t = x.shape[0]
    return pl.pallas_call(
        _router_kernel, grid=(t // tm,),
        in_specs=[pl.BlockSpec((tm, D), lambda i: (i, 0)), pl.BlockSpec((1, D), lambda i: (0, 0)),
                  pl.BlockSpec((1, N_MOD, D), lambda i: ((i * tm) // rows_per_mod, 0, 0)),
                  pl.BlockSpec((N_EXPERTS, D), lambda i: (0, 0))],
        out_specs=[pl.BlockSpec((tm, D), lambda i: (i, 0)), pl.BlockSpec((N_EXPERTS, tm), lambda i: (0, i))],
        out_shape=[jax.ShapeDtypeStruct((t, D), bf16), jax.ShapeDtypeStruct((N_EXPERTS, t), f32)],
        compiler_params=_cparams("arbitrary"), name="moe_router",
    )(x, g, mod, w_router_t)


THRESHOLD_REFINE_STEPS = 28
META_HI = 8
META_NROUND = 64


def _route_kernel(aff_ref, key_ref, start_ref, meta_ref, *, cap, window):
    t = aff_ref.shape[1]
    aff = aff_ref[...]

    def count_ge(v):
        return jnp.sum(jnp.where(aff >= v, 1.0, 0.0), axis=1, keepdims=True)

    bits = lax.bitcast_convert_type(aff, i32)
    tb = jnp.zeros((N_EXPERTS, 1), i32)
    for bit in range(30, -1, -1):
        cand = tb | (1 << bit)
        cnt = jnp.sum((bits >= cand).astype(i32), axis=1, keepdims=True)
        tb = jnp.where(cnt >= cap, cand, tb)
    approx = lax.bitcast_convert_type(tb, f32)
    lo = jnp.where(count_ge(0.5 * approx) >= cap, 0.5 * approx, 0.0)
    hi = jnp.where(count_ge(2.0 * approx) < cap, 2.0 * approx, 2.0)

    def refine(_, lh):
        lo, hi = lh
        mid = 0.5 * (lo + hi)
        ge = count_ge(mid) >= cap
        return jnp.where(ge, mid, lo), jnp.where(ge, hi, mid)

    thr, _ = lax.fori_loop(0, THRESHOLD_REFINE_STEPS, refine, (lo, hi))
    gt = aff > thr
    eq = aff == thr
    gtf = jnp.where(gt, 1.0, 0.0)
    eqf = jnp.where(eq, 1.0, 0.0)
    need = cap - jnp.sum(gtf, axis=1, keepdims=True).astype(i32)
    blk = TOKEN_BLOCK
    tri = (lax.broadcasted_iota(i32, (blk, blk), 0) <= lax.broadcasted_iota(i32, (blk, blk), 1))
    tri = jnp.where(tri, 1.0, 0.0).astype(bf16)
    carry = jnp.zeros((2 * N_EXPERTS, 1), f32)
    start_ref[...] = jnp.zeros_like(start_ref)
    meta_ref[...] = jnp.zeros_like(meta_ref)
    before = jnp.zeros((N_EXPERTS, 1), i32)
    nsub = cap // GATHER_TILE
    assert nsub <= META_HI
    first_blk = [jnp.zeros((N_EXPERTS, 1), i32) for _ in range(nsub)]
    last_blk = [jnp.zeros((N_EXPERTS, 1), i32) for _ in range(nsub)]
    for j in range(t // blk):
        sl = slice(j * blk, (j + 1) * blk)
        m = jnp.concatenate([gtf[:, sl], eqf[:, sl]], axis=0)
        pc = _dot(m.astype(bf16), tri) + carry
        carry = pc[:, blk - 1:blk]
        cs_gt = pc[:N_EXPERTS].astype(i32)
        cs_eq = pc[N_EXPERTS:].astype(i32)
        sel = gt[:, sl] | (eq[:, sl] & (cs_eq <= need))
        cs = cs_gt + jnp.minimum(cs_eq, need)
        key_ref[:, sl] = jnp.where(sel, cs, 0)
        end = cs[:, blk - 1:blk]
        start = before & ~7
        start_ref[:, j:j + 1] = start
        rounds = (end - start + (window - 1)) // window
        meta_ref[:, META_NROUND + j:META_NROUND + j + 1] = jnp.broadcast_to(
            jnp.max(rounds, axis=0, keepdims=True), (N_EXPERTS, 1))
        for s in range(nsub):
            first_blk[s] = first_blk[s] + jnp.where(end <= s * GATHER_TILE, 1, 0)
            last_blk[s] = last_blk[s] + jnp.where(end < (s + 1) * GATHER_TILE, 1, 0)
        before = end
    for s in range(nsub):
        meta_ref[:, s:s + 1] = first_blk[s]
        meta_ref[:, META_HI + s:META_HI + s + 1] = last_blk[s]


def moe_route(aff_t, cap, window):
    t = aff_t.shape[1]
    small = jax.ShapeDtypeStruct((N_EXPERTS, 128), i32)
    small_spec = pl.BlockSpec((N_EXPERTS, 128), lambda i: (0, 0))
    return pl.pallas_call(
        functools.partial(_route_kernel, cap=cap, window=window), grid=(1,),
        in_specs=[pl.BlockSpec((N_EXPERTS, t), lambda i: (0, 0))],
        out_specs=[pl.BlockSpec((N_EXPERTS, t), lambda i: (0, 0)), small_spec, small_spec],
        out_shape=[jax.ShapeDtypeStruct((N_EXPERTS, t), i32), small, small],
        compiler_params=_cparams("arbitrary"), name="moe_route",
    )(aff_t)


def _moe_ffn_kernel(meta_ref, key_ref, aff_ref, h_ref, wg_ref, wu_ref, wd_ref, o_ref, acc_ref, gate_ref, *, gb):
    e = pl.program_id(0)
    j = pl.program_id(1)
    tm = acc_ref.shape[0]
    tb = TOKEN_BLOCK
    acc_ref[...] = jnp.zeros_like(acc_ref)
    gate_ref[...] = jnp.zeros_like(gate_ref)
    gt = GATHER_TILE
    nsub = tm // gt
    nblk = key_ref.shape[1]
    row_id = lax.broadcasted_iota(i32, (gt, 1), 0)
    for s in range(nsub):
        sub = j * nsub + s
        srows = slice(s * gt, (s + 1) * gt)
        first = meta_ref[e, sub]
        last = meta_ref[e, META_HI + sub]

        def window(w, carry, sub=sub, srows=srows, first=first):
            begin = first + w * gb
            start = jnp.minimum(begin, nblk - gb)
            pieces = []
            gate = jnp.zeros((gt, 1), f32)
            for k in range(gb):
                b = start + k
                slot = jnp.where(b >= begin, sub * gt + 1, -gt) + row_id
                oh = key_ref[0, b] == slot
                pieces.append(jnp.where(oh, 1.0, 0.0).astype(bf16))
                gate = gate + jnp.sum(jnp.where(oh, aff_ref[0, b], 0.0), axis=1, keepdims=True)
            rows = pl.ds(pl.multiple_of(start * tb, tb), gb * tb)
            acc_ref[srows, :] += _dot(jnp.concatenate(pieces, axis=1), h_ref[rows, :])
            gate_ref[srows, :] += gate
            return carry

        lax.fori_loop(0, (last - first) // gb + 1, window, 0)
    xe = acc_ref[...].astype(bf16)
    hid = _silu(_dot(xe, wg_ref[0, 0].astype(bf16))) * _dot(xe, wu_ref[0, 0].astype(bf16))
    o_ref[0] = _dot(hid.astype(bf16), wd_ref[0, 0].astype(bf16)) * gate_ref[...]


def moe_ffn(meta, key, aff_t, h, wg, wu, wd, layer, cap, gather_blocks):
    t = h.shape[0]
    tm = SLOT_TILE
    nblk = t // TOKEN_BLOCK
    nj = cap // tm
    key4 = key.reshape(N_EXPERTS, nblk, 1, TOKEN_BLOCK)
    aff4 = aff_t.reshape(N_EXPERTS, nblk, 1, TOKEN_BLOCK)
    row_spec = pl.BlockSpec((1, nblk, 1, TOKEN_BLOCK), lambda e, j, *_: (e, 0, 0, 0))
    wspec = pl.BlockSpec((1, 1, D, D_FF), lambda e, j, *_: (layer, e, 0, 0))
    grid_spec = pltpu.PrefetchScalarGridSpec(
        num_scalar_prefetch=1, grid=(N_EXPERTS, nj),
        in_specs=[row_spec, row_spec,
                  pl.BlockSpec((t, D), lambda e, j, *_: (0, 0), pipeline_mode=pl.Buffered(1)),
                  wspec, wspec, pl.BlockSpec((1, 1, D_FF, D), lambda e, j, *_: (layer, e, 0, 0))],
        out_specs=pl.BlockSpec((1, tm, D), lambda e, j, *_: (e, j, 0)),
        scratch_shapes=[pltpu.VMEM((tm, D), f32), pltpu.VMEM((tm, 1), f32)])
    return pl.pallas_call(
        functools.partial(_moe_ffn_kernel, gb=gather_blocks), grid_spec=grid_spec,
        out_shape=jax.ShapeDtypeStruct((N_EXPERTS, cap, D), f32),
        compiler_params=_cparams("arbitrary", "arbitrary"), name="moe_ffn",
    )(meta, key4, aff4, h, wg, wu, wd)


def _moe_combine_kernel(start_ref, meta_ref, key_ref, ye_hbm, x_ref, mod_ref, fg_ref, o_ref, stage_ref, acc_ref, sem,
                        *, cap, nblk, window, final_norm):
    i = pl.program_id(0)
    w = window

    def window_start(tile, e, r):
        return start_ref[e, tile] + r * w

    def copies(tile, r, buf):
        out = []
        for e in range(N_EXPERTS):
            first = pl.multiple_of(jnp.minimum(window_start(tile, e, r), cap - w), 8)
            out.append(pltpu.make_async_copy(ye_hbm.at[e, pl.ds(first, w), :],
                                             stage_ref.at[buf, pl.ds(e * w, w), :], sem.at[buf]))
        return out

    def accumulate(r, buf):
        pieces = []
        for e in range(N_EXPERTS):
            begin = window_start(i, e, r)
            slot = jnp.minimum(begin, cap - w) + lax.broadcasted_iota(i32, (w, 1), 0) + 1
            oh = (key_ref[e:e + 1, :] == slot) & (slot > begin)
            pieces.append(jnp.where(oh, 1.0, 0.0))
        oh = jnp.concatenate(pieces, axis=0).astype(bf16)
        rows = stage_ref[buf]
        hi = rows.astype(bf16)
        lo = (rows - hi.astype(f32)).astype(bf16)
        acc_ref[...] += _dot_tn(oh, hi) + _dot_tn(oh, lo)

    cur = lax.rem(i, 2)

    @pl.when(i == 0)
    def _():
        for cp in copies(0, 0, 0):
            cp.start()

    @pl.when(i + 1 < nblk)
    def _():
        for cp in copies(i + 1, 0, 1 - cur):
            cp.start()

    acc_ref[...] = jnp.zeros_like(acc_ref)
    for cp in copies(i, 0, cur):
        cp.wait()
    accumulate(0, cur)

    def extra_round(r, carry):
        cps = copies(i, r, 2)
        for cp in cps:
            cp.start()
        for cp in cps:
            cp.wait()
        accumulate(r, 2)
        return carry

    lax.fori_loop(1, meta_ref[0, META_NROUND + i], extra_round, 0)
    out = x_ref[...] + mod_ref[0][GATE2:GATE2 + 1, :] * acc_ref[...]
    if final_norm:
        out = _rms(out, fg_ref[...])
    o_ref[...] = out


def moe_combine(starts, meta, key, ye, x, mod, final_g, cap, rows_per_mod, window, final_norm):
    t = x.shape[0]
    tb = TOKEN_BLOCK
    nblk = t // tb
    grid_spec = pltpu.PrefetchScalarGridSpec(
        num_scalar_prefetch=2, grid=(nblk,),
        in_specs=[pl.BlockSpec((N_EXPERTS, tb), lambda i, *_: (0, i)),
                  pl.BlockSpec(memory_space=pl.ANY),
                  pl.BlockSpec((tb, D), lambda i, *_: (i, 0)),
                  pl.BlockSpec((1, N_MOD, D), lambda i, *_: ((i * tb) // rows_per_mod, 0, 0)),
                  pl.BlockSpec((1, D), lambda i, *_: (0, 0))],
        out_specs=pl.BlockSpec((tb, D), lambda i, *_: (i, 0)),
        scratch_shapes=[pltpu.VMEM((3, N_EXPERTS * window, D), f32), pltpu.VMEM((tb, D), f32),
                        pltpu.SemaphoreType.DMA((3,))])
    return pl.pallas_call(
        functools.partial(_moe_combine_kernel, cap=cap, nblk=nblk, window=window, final_norm=final_norm),
        grid_spec=grid_spec,
        out_shape=jax.ShapeDtypeStruct((t, D), f32),
        compiler_params=_cparams("arbitrary"), name="moe_combine",
    )(starts, meta, key, ye, x, mod, final_g)


def moe_layer(x, g2, mod, w_router_t, wg, wu, wd, final_g, layer, rows_per_mod, window, gather_blocks):
    t = x.shape[0]
    cap = 2 * t // N_EXPERTS
    h, aff_t = moe_router(x, g2, mod, w_router_t, rows_per_mod)
    key, starts, meta = moe_route(aff_t, cap, window)
    ye = moe_ffn(meta, key, aff_t, h, wg, wu, wd, layer, cap, gather_blocks)
    return moe_combine(starts, meta, key, ye, x, mod, final_g, cap, rows_per_mod, window, layer == DEPTH - 1)


def _rope_tables(seq):
    rows = seq // GRID_W
    row = jnp.repeat(jnp.arange(rows), GRID_W).astype(f32)
    col = jnp.tile(jnp.arange(GRID_W), rows).astype(f32)
    pairs = QK_ROPE // 4
    inv = ROPE_THETA ** (-jnp.arange(pairs, dtype=f32) / pairs)
    ang = jnp.concatenate([row[:, None] * inv, col[:, None] * inv], axis=-1)
    cos, sin = jnp.cos(ang), jnp.sin(ang)
    zero = jnp.zeros_like(cos)
    c = jnp.concatenate([cos, cos, zero, zero], axis=-1)
    s1 = jnp.concatenate([-sin, zero, zero, zero], axis=-1)
    s2 = jnp.concatenate([zero, sin, zero, zero], axis=-1)
    return c, s1, s2


def _mla_weights(w_dq, q_norm_g, w_uq, w_dkv, kv_norm_g, w_ukv, w_o):
    per_head = QK_NOPE + QK_ROPE
    uq = w_uq.reshape(Q_LORA, MLA_HEADS, per_head)
    uq_nope = uq[..., :QK_NOPE].reshape(Q_LORA, MLA_HEADS * QK_NOPE)
    uq_rope = jnp.pad(uq[..., QK_NOPE:], ((0, 0), (0, 0), (0, HEAD_PAD - QK_ROPE))).reshape(Q_LORA, MLA_HEADS * HEAD_PAD)
    wuq = jnp.concatenate([uq_nope, uq_rope], axis=1).astype(bf16)
    wdkv = jnp.pad(w_dkv, ((0, 0), (0, HEAD_PAD - QK_ROPE))).astype(bf16)
    ukv = w_ukv.reshape(KV_LORA, MLA_HEADS, QK_NOPE + V_HEAD)
    wukv = jnp.concatenate([ukv[..., :QK_NOPE].reshape(KV_LORA, -1), ukv[..., QK_NOPE:].reshape(KV_LORA, -1)], axis=1).astype(bf16)
    proj = (w_dq.astype(bf16), q_norm_g.reshape(1, Q_LORA), wuq, wdkv, kv_norm_g.reshape(1, KV_LORA), wukv)
    return proj, wukv, w_o.astype(bf16)


def _mla_layer(xp, xs, modp, mods, g1, cache_ckv_j, cache_krope_j, w, rope_tabs):
    proj_w, wukv, w_o = w
    bp, bs = xp.shape[0] // 256, xs.shape[0] // 2048
    nh = MLA_HEADS * HEAD_PAD
    qn, qr, kn, v, kr, ckv, krope = mla_project(xp, g1, modp, proj_w, None, xp.shape[0])
    r3 = lambda a, b: a.reshape(b, -1, a.shape[-1])
    op = mla_attention(r3(qn, bp), r3(qr, bp), [(r3(kn, bp), r3(kr, bp), r3(v, bp))], nb=2).reshape(-1, nh)
    xp = matmul_residual(op, w_o, xp, modp, GATE1, xp.shape[0])
    new_ckv = ckv.reshape(bp, -1, KV_LORA)
    new_krope = krope[:, :QK_ROPE].reshape(bp, -1, QK_ROPE)
    qn, qr, kn, v, kr, _, _ = mla_project(xs, g1, mods, proj_w, rope_tabs, 2048)
    ctx = matmul_bf16(cache_ckv_j.reshape(-1, KV_LORA).astype(bf16), wukv, tm=512)
    nk = MLA_HEADS * QK_NOPE
    kr_ctx = jnp.pad(cache_krope_j, ((0, 0), (0, 0), (0, HEAD_PAD - QK_ROPE))).astype(bf16)
    segments = [(ctx[:, :nk].reshape(bs, -1, nk), kr_ctx, ctx[:, nk:].reshape(bs, -1, nk)),
                (r3(kn, bs), r3(kr, bs), r3(v, bs))]
    os_ = mla_attention(r3(qn, bs), r3(qr, bs), segments).reshape(-1, nh)
    xs = matmul_residual(os_, w_o, xs, mods, GATE1, 2048)
    return xp, xs, new_ckv, new_krope


def _ssm_stream(x, mod, g1, w, rows_per_mod, seq, h0, want_state=False, prev_states=None):
    w_all, conv_w_all, bias_all, a_log, d_skip_row, norm_g, w_out = w
    b = x.shape[0] // seq
    z, xbc, dt = ssm_in_proj(x, g1, mod, w_all, conv_w_all, bias_all, rows_per_mod, seq)
    dt4 = dt[:, :2 * SSM_H].reshape(b, seq, 2, SSM_H)
    dt_dir = dt4.transpose(2, 0, 1, 3)
    dtT_dir = dt4.transpose(2, 0, 3, 1)
    xbc3 = xbc.reshape(b, seq, CONV_DIM)
    bT = xbc3[:, :, D_INNER:D_INNER + SSM_G * SSM_N].transpose(0, 2, 1)
    outs = ssd_scan(xbc3, z.reshape(b, seq, D_INNER), dt_dir, dtT_dir, bT, a_log, d_skip_row, norm_g, h0,
                    want_state, prev_states, nb=2 if seq <= 2 * CHUNK else 1)
    x = matmul_residual(outs[0].reshape(-1, D_INNER), w_out, x, mod, GATE1, rows_per_mod)
    return x, (outs[1] if want_state else None)


def kernel(x_prompt, x_sample, cache_ckv, cache_krope, state_ssm, c, c_ctx, w_mod, b_mod, norm1_g, norm2_g, final_norm_g, mla_w_dq, mla_q_norm_g, mla_w_uq, mla_w_dkv, mla_kv_norm_g, mla_w_ukv, mla_w_o, ssm_w_in, ssm_conv_w, ssm_conv_b, ssm_dt_bias, ssm_a_log, ssm_d_skip, ssm_norm_g, ssm_w_out, moe_w_router, moe_w_gate, moe_w_up, moe_w_down):
    bp, lp, _ = x_prompt.shape
    bs, ls, _ = x_sample.shape
    xp = x_prompt.reshape(bp * lp, D)
    xs = x_sample.reshape(bs * ls, D)

    cond8 = jnp.concatenate([c_ctx[None, :], c, jnp.zeros((8 - 1 - bs, D), f32)], axis=0)
    mod_all = modulation_all(cond8, w_mod, b_mod)
    rope_tabs = _rope_tables(ls)

    new_ckv, new_krope = [], []
    new_ssm = None
    for l in range(DEPTH):
        modp = mod_all[l, 0:1].reshape(1, N_MOD, D)
        mods = mod_all[l, 1:1 + bs].reshape(bs, N_MOD, D)
        g1 = norm1_g[l].reshape(1, D)
        j = l // 2
        if l % 2 == 0:
            w = _mla_weights(mla_w_dq[j], mla_q_norm_g[j], mla_w_uq[j], mla_w_dkv[j], mla_kv_norm_g[j], mla_w_ukv[j], mla_w_o[j])
            xp, xs, ckv, krope = _mla_layer(xp, xs, modp, mods, g1, cache_ckv[:, j], cache_krope[:, j], w, rope_tabs)
            new_ckv.append(ckv)
            new_krope.append(krope)
        else:
            dt_cols = INPROJ_BLOCK - 2 * SSM_H
            w_all = jnp.pad(ssm_w_in[j], ((0, 0), (0, dt_cols))).astype(bf16)
            conv_w_all = jnp.pad(ssm_conv_w[j].T, ((0, 0), (D_INNER, INPROJ_BLOCK)))
            bias_all = jnp.concatenate([jnp.zeros((D_INNER,), f32), ssm_conv_b[j], ssm_dt_bias[j].reshape(-1),
                                        jnp.zeros((dt_cols,), f32)]).reshape(1, -1)
            w = (w_all, conv_w_all, bias_all, ssm_a_log[j],
                 jnp.repeat(ssm_d_skip[j], SSM_P).reshape(1, D_INNER), ssm_norm_g[j].reshape(1, D_INNER),
                 ssm_w_out[j].astype(bf16))
            xp, new_ssm = _ssm_stream(xp, modp, g1, w, bp * lp, lp, None, want_state=True, prev_states=new_ssm)
            h0 = state_ssm[:, j].transpose(0, 1, 4, 2, 3).reshape(bs, 2, SSM_N, D_INNER)
            xs, _ = _ssm_stream(xs, mods, g1, w, ls, ls, h0)
        g2 = norm2_g[l].reshape(1, D)
        moe_w = (moe_w_router[l].T, moe_w_gate, moe_w_up, moe_w_down, final_norm_g.reshape(1, D), l)
        xp = moe_layer(xp, g2, modp, *moe_w, bp * lp, COMBINE_WINDOW_PROMPT, GATHER_BLOCKS_PROMPT)
        xs = moe_layer(xs, g2, mods, *moe_w, ls, COMBINE_WINDOW_SAMPLE, GATHER_BLOCKS_SAMPLE)

    y_prompt = xp.reshape(bp, lp, D)
    y_sample = xs.reshape(bs, ls, D)
    new_ssm = new_ssm.reshape(bp, DEPTH // 2, 2, SSM_H, SSM_P, SSM_N)
    return (y_prompt, y_sample, jnp.stack(new_ckv, axis=1), jnp.stack(new_krope, axis=1), new_ssm)
```

```python
import functools

import jax
import jax.numpy as jnp
from jax import lax
from jax.experimental import pallas as pl
from jax.experimental.pallas import tpu as pltpu

f32 = jnp.float32
bf16 = jnp.bfloat16
i32 = jnp.int32

D = 1024
DEPTH = 4
N_MOD = 6
EPS = 1e-6
GRID_W = 64

MLA_HEADS = 8
QK_NOPE = 128
QK_ROPE = 64
V_HEAD = 128
Q_LORA = 384
KV_LORA = 256
ROPE_THETA = 10000.0
ATTN_SCALE = (QK_NOPE + QK_ROPE) ** -0.5
HEAD_PAD = 128

D_INNER = 2 * D
SSM_P = 64
SSM_H = D_INNER // SSM_P
SSM_G = 4
SSM_N = 128
CHUNK = 128
CONV_DIM = D_INNER + 2 * SSM_G * SSM_N
HEADS_PER_GROUP = SSM_H // SSM_G
GROUP_W = HEADS_PER_GROUP * SSM_P

N_EXPERTS = 16
D_FF = 1024
TOKEN_BLOCK = 256
SLOT_TILE = 512
GATHER_TILE = 128
GATHER_BLOCKS_PROMPT = 8
GATHER_BLOCKS_SAMPLE = 8
COMBINE_WINDOW_PROMPT = 64
COMBINE_WINDOW_SAMPLE = 128

VMEM_LIMIT = 56 * 1024 * 1024

SHIFT1, SCALE1, GATE1, SHIFT2, SCALE2, GATE2 = range(6)


def _cparams(*sem):
    return pltpu.CompilerParams(dimension_semantics=sem, vmem_limit_bytes=VMEM_LIMIT)


def _dot(a, b):
    return jnp.dot(a, b, preferred_element_type=f32)


def _dot_nt(a, b):
    return lax.dot_general(a, b, (((1,), (1,)), ((), ())), preferred_element_type=f32)


def _dot_tn(a, b):
    return lax.dot_general(a, b, (((0,), (0,)), ((), ())), preferred_element_type=f32)


def _dot_f32(a, b):
    return jnp.dot(a, b, preferred_element_type=f32, precision=lax.Precision.HIGHEST)


def _rms(x, g):
    ms = jnp.mean(x * x, axis=-1, keepdims=True)
    return x * lax.rsqrt(ms + EPS) * g


def _norm_mod(x, g, mod, k_shift, k_scale):
    return _rms(x, g) * (1.0 + mod[k_scale:k_scale + 1, :]) + mod[k_shift:k_shift + 1, :]


def _silu(x):
    return x * jax.nn.sigmoid(x)


def _mod_kernel(c_ref, w_ref, b_ref, o_ref):
    s = _silu(c_ref[...]).astype(bf16)
    o_ref[0] = _dot(s, w_ref[0].astype(bf16)) + b_ref[0]


def modulation_all(cond8, w_mod, b_mod):
    nb = 1536
    n = N_MOD * D
    return pl.pallas_call(
        _mod_kernel,
        grid=(DEPTH, n // nb),
        in_specs=[
            pl.BlockSpec((8, D), lambda l, j: (0, 0)),
            pl.BlockSpec((1, D, nb), lambda l, j: (l, 0, j)),
            pl.BlockSpec((1, 1, nb), lambda l, j: (l, 0, j)),
        ],
        out_specs=pl.BlockSpec((1, 8, nb), lambda l, j: (l, 0, j)),
        out_shape=jax.ShapeDtypeStruct((DEPTH, 8, n), f32),
        compiler_params=_cparams("arbitrary", "arbitrary"),
        name="modulation",
    )(cond8, w_mod, b_mod.reshape(DEPTH, 1, n))


def _rope_rot(p, c, s1, s2):
    return p * c + pltpu.roll(p, 96, axis=1) * s1 + pltpu.roll(p, 32, axis=1) * s2


def _mla_proj_kernel(*refs, rope):
    if rope:
        (x_ref, g_ref, mod_ref, wdq_ref, qg_ref, wuq_ref, wdkv_ref, kvg_ref, wukv_ref, rc_ref, rs1_ref, rs2_ref,
         qn_ref, qr_ref, kn_ref, v_ref, kr_ref, ckv_ref, krope_ref) = refs
    else:
        (x_ref, g_ref, mod_ref, wdq_ref, qg_ref, wuq_ref, wdkv_ref, kvg_ref, wukv_ref,
         qn_ref, qr_ref, kn_ref, v_ref, kr_ref, ckv_ref, krope_ref) = refs
    h = _norm_mod(x_ref[...], g_ref[...], mod_ref[0], SHIFT1, SCALE1).astype(bf16)
    nq = MLA_HEADS * QK_NOPE
    q_lat = _rms(_dot(h, wdq_ref[...]), qg_ref[...]).astype(bf16)
    q = _dot(q_lat, wuq_ref[...])
    qn_ref[...] = q[:, :nq].astype(bf16)
    if rope:
        c, s1, s2 = rc_ref[...], rs1_ref[...], rs2_ref[...]
    for hh in range(MLA_HEADS):
        piece = q[:, nq + hh * HEAD_PAD: nq + (hh + 1) * HEAD_PAD]
        if rope:
            piece = _rope_rot(piece, c, s1, s2)
        qr_ref[:, hh * HEAD_PAD:(hh + 1) * HEAD_PAD] = piece.astype(bf16)
    kv = _dot(h, wdkv_ref[...])
    ckv = _rms(kv[:, :KV_LORA], kvg_ref[...])
    ckv_ref[...] = ckv
    kr = kv[:, KV_LORA:]
    krope_ref[...] = kr
    if rope:
        kr = _rope_rot(kr, c, s1, s2)
    kr_ref[...] = kr.astype(bf16)
    kvx = _dot(ckv.astype(bf16), wukv_ref[...])
    nk = MLA_HEADS * QK_NOPE
    kn_ref[...] = kvx[:, :nk].astype(bf16)
    v_ref[...] = kvx[:, nk:].astype(bf16)


def mla_project(x, g, mod, w, rope_tabs, rows_per_mod, tm=256):
    t = x.shape[0]
    wdq, qg, wuq, wdkv, kvg, wukv = w
    full = lambda a: pl.BlockSpec(a.shape, lambda i: (0,) * a.ndim)
    row = lambda n: pl.BlockSpec((tm, n), lambda i: (i, 0))
    in_specs = [row(D), full(g), pl.BlockSpec((1, N_MOD, D), lambda i: ((i * tm) // rows_per_mod, 0, 0)),
                full(wdq), full(qg), full(wuq), full(wdkv), full(kvg), full(wukv)]
    args = [x, g, mod, wdq, qg, wuq, wdkv, kvg, wukv]
    rope = rope_tabs is not None
    if rope:
        nrb = rope_tabs[0].shape[0] // tm
        in_specs += [pl.BlockSpec((tm, HEAD_PAD), lambda i: (i % nrb, 0))] * 3
        args += list(rope_tabs)
    nh = MLA_HEADS * HEAD_PAD
    out_shape = [jax.ShapeDtypeStruct((t, nh), bf16)] * 4 + [
        jax.ShapeDtypeStruct((t, HEAD_PAD), bf16), jax.ShapeDtypeStruct((t, KV_LORA), f32),
        jax.ShapeDtypeStruct((t, HEAD_PAD), f32)]
    out_specs = [row(nh)] * 4 + [row(HEAD_PAD), row(KV_LORA), row(HEAD_PAD)]
    return pl.pallas_call(
        functools.partial(_mla_proj_kernel, rope=rope),
        grid=(t // tm,), in_specs=in_specs, out_specs=out_specs, out_shape=out_shape,
        compiler_params=_cparams("arbitrary"), name="mla_project",
    )(*args)


def _mm_kernel(a_ref, w_ref, o_ref):
    o_ref[...] = _dot(a_ref[...], w_ref[...]).astype(o_ref.dtype)


def matmul_bf16(a, w, tm):
    m, k = a.shape
    n = w.shape[1]
    return pl.pallas_call(
        _mm_kernel, grid=(m // tm,),
        in_specs=[pl.BlockSpec((tm, k), lambda i: (i, 0)), pl.BlockSpec((k, n), lambda i: (0, 0))],
        out_specs=pl.BlockSpec((tm, n), lambda i: (i, 0)),
        out_shape=jax.ShapeDtypeStruct((m, n), bf16),
        compiler_params=_cparams("arbitrary"), name="matmul_bf16",
    )(a, w)


def _attn_kernel(*refs, nseg):
    qn_ref, qr_ref = refs[:2]
    segs = [refs[2 + 3 * i: 5 + 3 * i] for i in range(nseg)]
    o_ref, kcat_ref = refs[2 + 3 * nseg:]
    nb = qn_ref.shape[0]
    offs = [0]
    for kn_ref, _, _ in segs:
        offs.append(offs[-1] + kn_ref.shape[1])

    @pl.when(pl.program_id(1) == 0)
    def _():
        for b in range(nb):
            for (kn_ref, kr_ref, _), off in zip(segs, offs):
                rows = slice(off, off + kn_ref.shape[1])
                for h in range(MLA_HEADS):
                    kcat_ref[b, h, rows, :HEAD_PAD] = kn_ref[b, :, h * HEAD_PAD:(h + 1) * HEAD_PAD]
                    kcat_ref[b, h, rows, HEAD_PAD:] = kr_ref[b]

    for b in range(nb):
        for h in range(MLA_HEADS):
            sl = slice(h * HEAD_PAD, (h + 1) * HEAD_PAD)
            q = jnp.concatenate([qn_ref[b, :, sl], qr_ref[b, :, sl]], axis=1)
            s = _dot_nt(q, kcat_ref[b, h]) * ATTN_SCALE
            e = jnp.exp(s - jnp.max(s, axis=-1, keepdims=True))
            l = jnp.sum(e, axis=-1, keepdims=True)
            p = e.astype(bf16)
            o = sum(_dot(p[:, off:off + v_ref.shape[1]], v_ref[b, :, sl]) for (_, _, v_ref), off in zip(segs, offs))
            o_ref[b, :, sl] = (o / l).astype(bf16)


def mla_attention(qn, qr, kv_segments, tq=256, nb=1):
    b, lq, nh = qn.shape
    qspec = pl.BlockSpec((nb, tq, nh), lambda i, j: (i, j, 0))
    in_specs, args = [qspec, qspec], [qn, qr]
    for kn, kr, v in kv_segments:
        s = kn.shape[1]
        kspec = pl.BlockSpec((nb, s, nh), lambda i, j: (i, 0, 0))
        in_specs += [kspec, pl.BlockSpec((nb, s, HEAD_PAD), lambda i, j: (i, 0, 0)), kspec]
        args += [kn, kr, v]
    s_all = sum(kn.shape[1] for kn, _, _ in kv_segments)
    return pl.pallas_call(
        functools.partial(_attn_kernel, nseg=len(kv_segments)), grid=(b // nb, lq // tq),
        in_specs=in_specs, out_specs=qspec, out_shape=jax.ShapeDtypeStruct((b, lq, nh), bf16),
        scratch_shapes=[pltpu.VMEM((nb, MLA_HEADS, s_all, 2 * HEAD_PAD), bf16)],
        compiler_params=_cparams("arbitrary", "arbitrary"), name="mla_attention",
    )(*args)


def _mm_res_kernel(a_ref, w_ref, x_ref, mod_ref, o_ref, *, kgate):
    y = _dot(a_ref[...], w_ref[...])
    o_ref[...] = x_ref[...] + mod_ref[0][kgate:kgate + 1, :] * y


def matmul_residual(a, w, x, mod, kgate, rows_per_mod, tm=512):
    t, k = a.shape
    return pl.pallas_call(
        functools.partial(_mm_res_kernel, kgate=kgate), grid=(t // tm,),
        in_specs=[pl.BlockSpec((tm, k), lambda i: (i, 0)), pl.BlockSpec((k, D), lambda i: (0, 0)),
                  pl.BlockSpec((tm, D), lambda i: (i, 0)),
                  pl.BlockSpec((1, N_MOD, D), lambda i: ((i * tm) // rows_per_mod, 0, 0))],
        out_specs=pl.BlockSpec((tm, D), lambda i: (i, 0)),
        out_shape=jax.ShapeDtypeStruct((t, D), f32),
        compiler_params=_cparams("arbitrary"), name="matmul_residual",
    )(a, w, x, mod)


INPROJ_BLOCK = 512
INPROJ_Z_BLOCKS = D_INNER // INPROJ_BLOCK
INPROJ_CONV_BLOCKS = CONV_DIM // INPROJ_BLOCK
DT_PAD = 128


def _inproj_kernel(x_ref, g_ref, mod_ref, w_ref, cw_ref, b_ref, z_ref, xbc_ref, dt_ref, h_ref, *, period):
    j = pl.program_id(1)

    @pl.when(j == 0)
    def _():
        h_ref[...] = _norm_mod(x_ref[...], g_ref[...], mod_ref[0], SHIFT1, SCALE1).astype(bf16)

    @pl.when(j < INPROJ_Z_BLOCKS)
    def _():
        z_ref[...] = _dot(h_ref[...], w_ref[...])

    @pl.when((j >= INPROJ_Z_BLOCKS) & (j < INPROJ_Z_BLOCKS + INPROJ_CONV_BLOCKS))
    def _():
        y = _dot(h_ref[...], w_ref[...])
        tm = y.shape[0]
        pos = lax.broadcasted_iota(i32, (tm, 1), 0) & (period - 1)
        prev = jnp.where(pos == 0, 0.0, pltpu.roll(y, 1, axis=0))
        nxt = jnp.where(pos == period - 1, 0.0, pltpu.roll(y, tm - 1, axis=0))
        cw = cw_ref[...]
        xbc_ref[...] = _silu(cw[0:1, :] * prev + cw[1:2, :] * y + cw[2:3, :] * nxt + b_ref[...])

    @pl.when(j == INPROJ_Z_BLOCKS + INPROJ_CONV_BLOCKS)
    def _():
        y = _dot(h_ref[...], w_ref[:, :DT_PAD]) + b_ref[:, :DT_PAD]
        dt_ref[...] = jnp.maximum(y, 0.0) + jnp.log1p(jnp.exp(-jnp.abs(y)))


def ssm_in_proj(x, g, mod, w_all, conv_w_all, bias_all, rows_per_mod, period, tm=2048):
    t = x.shape[0]
    nb = INPROJ_BLOCK
    nz, nx = INPROJ_Z_BLOCKS, INPROJ_CONV_BLOCKS
    col = lambda rows: pl.BlockSpec((rows, nb), lambda i, j: (0, j))
    return pl.pallas_call(
        functools.partial(_inproj_kernel, period=period), grid=(t // tm, nz + nx + 1),
        in_specs=[pl.BlockSpec((tm, D), lambda i, j: (i, 0)), pl.BlockSpec((1, D), lambda i, j: (0, 0)),
                  pl.BlockSpec((1, N_MOD, D), lambda i, j: ((i * tm) // rows_per_mod, 0, 0)),
                  col(D), col(3), col(1)],
        out_specs=[pl.BlockSpec((tm, nb), lambda i, j: (i, jnp.minimum(j, nz - 1))),
                   pl.BlockSpec((tm, nb), lambda i, j: (i, jnp.clip(j - nz, 0, nx - 1))),
                   pl.BlockSpec((tm, DT_PAD), lambda i, j: (i, 0))],
        out_shape=[jax.ShapeDtypeStruct((t, D_INNER), f32), jax.ShapeDtypeStruct((t, CONV_DIM), f32),
                   jax.ShapeDtypeStruct((t, DT_PAD), f32)],
        scratch_shapes=[pltpu.VMEM((tm, D), bf16)],
        compiler_params=_cparams("arbitrary", "arbitrary"), name="ssm_in_proj",
    )(x, g, mod, w_all, conv_w_all, bias_all)


def _ssd_kernel(*refs, nc, has_h0, has_hout, n_prev):
    xbc_ref, z_ref, dt_ref, dtT_ref, bT_ref, alr_ref, alc_ref, dsk_ref, ng_ref = refs[:9]
    rest = list(refs[9:])
    h0_ref = rest.pop(0) if has_h0 else None
    prev_ref = rest.pop(0) if n_prev else None
    y_ref = rest.pop(0)
    hout_ref = rest.pop(0) if has_hout else None
    state_all, yf_all, yc_all, col_all, row_all = rest
    q = CHUNK
    nb = xbc_ref.shape[0]
    d = pl.program_id(1)
    c = pl.program_id(2)
    ce = jnp.where(d == 0, c, nc - 1 - c)

    @pl.when(c == 0)
    def _():
        if has_h0:
            state_all[...] = h0_ref[:, 0]
        else:
            state_all[...] = jnp.zeros_like(state_all)

    ii = lax.broadcasted_iota(i32, (q, q), 0)
    jj = lax.broadcasted_iota(i32, (q, q), 1)
    ahead = (ii - jj) * jnp.where(d == 0, 1, -1)
    causal = ahead >= 0
    first_head = lax.broadcasted_iota(i32, (1, 2 * SSM_P), 1) < SSM_P
    m_col = jnp.where(causal, 1.0, 0.0).astype(f32)
    m_row = jnp.where(ahead <= 0, 1.0, 0.0).astype(f32)

    a_row = -jnp.exp(alr_ref[0])
    a_col = -jnp.exp(alc_ref[0])
    rows = pl.ds(pl.multiple_of(ce * q, q), q)
    for bi in range(nb):
        _ssd_decays(bi, m_col, m_row, a_row, a_col, dt_ref, dtT_ref, col_all.at[bi], row_all.at[bi])
    c_off = D_INNER + SSM_G * SSM_N
    for g in range(SSM_G):
        grp = []
        for bi in range(nb):
            b_g = xbc_ref[bi, :, D_INNER + g * SSM_N: D_INNER + (g + 1) * SSM_N].astype(bf16)
            c_g = xbc_ref[bi, :, c_off + g * SSM_N: c_off + (g + 1) * SSM_N].astype(bf16)
            grp.append((c_g, _dot_nt(c_g, b_g), bT_ref[bi, g * SSM_N:(g + 1) * SSM_N, :]))
        for pr in range(HEADS_PER_GROUP // 2):
            for bi in range(nb):
                _ssd_head_pair(bi, g * HEADS_PER_GROUP + 2 * pr, causal, first_head, *grp[bi], xbc_ref,
                               state_all.at[bi], yc_all.at[bi], col_all.at[bi], row_all.at[bi])

    @pl.when(d == 0)
    def _():
        yf_all[:, rows, :] = yc_all[...]

    @pl.when(d == 1)
    def _():
        for bi in range(nb):
            y = yf_all[bi, rows, :] + yc_all[bi] + dsk_ref[...] * xbc_ref[bi, :, :D_INNER]
            y = y * _silu(z_ref[bi])
            y_ref[bi] = _rms(y, ng_ref[...]).astype(bf16)

    if has_hout:
        @pl.when(c == nc - 1)
        def _():
            for bi in range(nb):
                if n_prev:
                    hout_ref[bi, :n_prev, 0] = prev_ref[bi, :, 0]
                hout_ref[bi, n_prev, 0] = state_all[bi].T


def _ssd_decays(bi, m_col, m_row, a_row, a_col, dt_ref, dtT_ref, col_ref, row_ref):
    q = CHUNK
    dt = dt_ref[0, bi]
    dtT = dtT_ref[0, bi]
    daT = dtT * a_col
    cs_col = _dot_f32(m_col, dt * a_row)
    cs_row = _dot_f32(daT, m_row)
    tot = jnp.sum(daT, axis=1, keepdims=True)
    col_ref[0] = cs_col
    col_ref[1] = jnp.exp(cs_col)
    row_ref[0] = cs_row
    row_ref[1] = dtT
    row_ref[2] = jnp.exp(tot - cs_row) * dtT
    row_ref[3] = jnp.broadcast_to(jnp.exp(tot), (SSM_H, q))


def _ssd_head_pair(bi, h0, causal, first_head, c_g, cb, bT_g, xbc_ref, state_ref, yc_ref, col_ref, row_ref):
    h1 = h0 + 1
    ps = slice(h0 * SSM_P, (h1 + 1) * SSM_P)
    x2 = xbc_ref[bi, :, ps]
    x_blk = jnp.concatenate([jnp.where(first_head, x2, 0.0), jnp.where(first_head, 0.0, x2)], axis=0).astype(bf16)
    st = state_ref[:, ps]
    ws, bs = [], []
    for h in (h0, h1):
        seg = col_ref[0, :, h:h + 1] - row_ref[0, h:h + 1, :]
        w = cb * jnp.exp(jnp.where(causal, seg, -jnp.inf)) * row_ref[1, h:h + 1, :]
        ws.append(w.astype(bf16))
        bs.append((bT_g * row_ref[2, h:h + 1, :]).astype(bf16))
    e_cs = jnp.where(first_head, col_ref[1, :, h0:h0 + 1], col_ref[1, :, h1:h1 + 1])
    yc_ref[:, ps] = _dot(jnp.concatenate(ws, axis=1), x_blk) + _dot(c_g, st.astype(bf16)) * e_cs
    dec = jnp.where(first_head, row_ref[3, h0:h0 + 1, :], row_ref[3, h1:h1 + 1, :])
    state_ref[:, ps] = st * dec + _dot(jnp.concatenate(bs, axis=1), x_blk)


def ssd_scan(xbc, z, dt_dir, dtT_dir, bT, a_log, d_skip_row, norm_g, h0, want_state=False, prev_states=None, nb=1):
    b, l, _ = xbc.shape
    nc = l // CHUNK
    q = CHUNK
    has_h0 = h0 is not None
    has_hout = want_state
    n_prev = 0 if prev_states is None else prev_states.shape[1]
    ce = lambda d, c: jnp.where(d == 0, c, nc - 1 - c)
    late = lambda d, c: jnp.where(d == 0, nc - 1, nc - 1 - c)
    in_specs = [
        pl.BlockSpec((nb, q, CONV_DIM), lambda i, d, c: (i, ce(d, c), 0)),
        pl.BlockSpec((nb, q, D_INNER), lambda i, d, c: (i, late(d, c), 0)),
        pl.BlockSpec((1, nb, q, SSM_H), lambda i, d, c: (d, i, ce(d, c), 0)),
        pl.BlockSpec((1, nb, SSM_H, q), lambda i, d, c: (d, i, 0, ce(d, c))),
        pl.BlockSpec((nb, SSM_G * SSM_N, q), lambda i, d, c: (i, 0, ce(d, c))),
        pl.BlockSpec((1, 1, SSM_H), lambda i, d, c: (d, 0, 0)),
        pl.BlockSpec((1, SSM_H, 1), lambda i, d, c: (d, 0, 0)),
        pl.BlockSpec((1, D_INNER), lambda i, d, c: (0, 0)),
        pl.BlockSpec((1, D_INNER), lambda i, d, c: (0, 0)),
    ]
    args = [xbc, z, dt_dir, dtT_dir, bT, a_log.reshape(2, 1, SSM_H), a_log.reshape(2, SSM_H, 1), d_skip_row, norm_g]
    if has_h0:
        in_specs.append(pl.BlockSpec((nb, 1, SSM_N, D_INNER), lambda i, d, c: (i, d, 0, 0)))
        args.append(h0)
    out_specs = [pl.BlockSpec((nb, q, D_INNER), lambda i, d, c: (i, late(d, c), 0))]
    out_shape = [jax.ShapeDtypeStruct((b, l, D_INNER), bf16)]
    if has_hout:
        if n_prev:
            in_specs.append(pl.BlockSpec((nb, n_prev, 1, D_INNER, SSM_N), lambda i, d, c: (i, 0, d, 0, 0)))
            args.append(prev_states)
        out_specs.append(pl.BlockSpec((nb, n_prev + 1, 1, D_INNER, SSM_N), lambda i, d, c: (i, 0, d, 0, 0)))
        out_shape.append(jax.ShapeDtypeStruct((b, n_prev + 1, 2, D_INNER, SSM_N), f32))
    return pl.pallas_call(
        functools.partial(_ssd_kernel, nc=nc, has_h0=has_h0, has_hout=has_hout, n_prev=n_prev),
        grid=(b // nb, 2, nc),
        in_specs=in_specs, out_specs=out_specs, out_shape=out_shape,
        scratch_shapes=[
            pltpu.VMEM((nb, SSM_N, D_INNER), f32),
            pltpu.VMEM((nb, l, D_INNER), f32),
            pltpu.VMEM((nb, q, D_INNER), f32),
            pltpu.VMEM((nb, 2, q, SSM_H), f32),
            pltpu.VMEM((nb, 4, SSM_H, q), f32),
        ],
        compiler_params=_cparams("arbitrary", "arbitrary", "arbitrary"), name="ssd_scan",
    )(*args)


def _router_kernel(x_ref, g_ref, mod_ref, wr_ref, h_ref, aff_ref):
    h = _norm_mod(x_ref[...], g_ref[...], mod_ref[0], SHIFT2, SCALE2)
    hb = h.astype(bf16)
    h_ref[...] = hb
    h_lo = (h - hb.astype(f32)).astype(bf16)
    w = wr_ref[...]
    w_hi = w.astype(bf16)
    w_lo = (w - w_hi.astype(f32)).astype(bf16)
    lg = _dot_nt(w_hi, hb) + _dot_nt(w_hi, h_lo) + _dot_nt(w_lo, hb)
    e = jnp.exp(lg - jnp.max(lg, axis=0, keepdims=True))
    aff_ref[...] = e / jnp.sum(e, axis=0, keepdims=True)


def moe_router(x, g, mod, w_router_t, rows_per_mod, tm=512):
    t = x.shape[0]
    return pl.pallas_call(
        _router_kernel, grid=(t // tm,),
        in_specs=[pl.BlockSpec((tm, D), lambda i: (i, 0)), pl.BlockSpec((1, D), lambda i: (0, 0)),
                  pl.BlockSpec((1, N_MOD, D), lambda i: ((i * tm) // rows_per_mod, 0, 0)),
                  pl.BlockSpec((N_EXPERTS, D), lambda i: (0, 0))],
        out_specs=[pl.BlockSpec((tm, D), lambda i: (i, 0)), pl.BlockSpec((N_EXPERTS, tm), lambda i: (0, i))],
        out_shape=[jax.ShapeDtypeStruct((t, D), bf16), jax.ShapeDtypeStruct((N_EXPERTS, t), f32)],
        compiler_params=_cparams("arbitrary"), name="moe_router",
    )(x, g, mod, w_router_t)


THRESHOLD_REFINE_STEPS = 28
META_HI = 8
META_NROUND = 64


def _route_kernel(aff_ref, key_ref, start_ref, meta_ref, *, cap, window):
    t = aff_ref.shape[1]
    aff = aff_ref[...]

    def count_ge(v):
        return jnp.sum(jnp.where(aff >= v, 1.0, 0.0), axis=1, keepdims=True)

    bits = lax.bitcast_convert_type(aff, i32)
    tb = jnp.zeros((N_EXPERTS, 1), i32)
    for bit in range(30, -1, -1):
        cand = tb | (1 << bit)
        cnt = jnp.sum((bits >= cand).astype(i32), axis=1, keepdims=True)
        tb = jnp.where(cnt >= cap, cand, tb)
    approx = lax.bitcast_convert_type(tb, f32)
    lo = jnp.where(count_ge(0.5 * approx) >= cap, 0.5 * approx, 0.0)
    hi = jnp.where(count_ge(2.0 * approx) < cap, 2.0 * approx, 2.0)

    def refine(_, lh):
        lo, hi = lh
        mid = 0.5 * (lo + hi)
        ge = count_ge(mid) >= cap
        return jnp.where(ge, mid, lo), jnp.where(ge, hi, mid)

    thr, _ = lax.fori_loop(0, THRESHOLD_REFINE_STEPS, refine, (lo, hi))
    gt = aff > thr
    eq = aff == thr
    gtf = jnp.where(gt, 1.0, 0.0)
    eqf = jnp.where(eq, 1.0, 0.0)
    need = cap - jnp.sum(gtf, axis=1, keepdims=True).astype(i32)
    blk = TOKEN_BLOCK
    tri = (lax.broadcasted_iota(i32, (blk, blk), 0) <= lax.broadcasted_iota(i32, (blk, blk), 1))
    tri = jnp.where(tri, 1.0, 0.0).astype(bf16)
    carry = jnp.zeros((2 * N_EXPERTS, 1), f32)
    start_ref[...] = jnp.zeros_like(start_ref)
    meta_ref[...] = jnp.zeros_like(meta_ref)
    before = jnp.zeros((N_EXPERTS, 1), i32)
    nsub = cap // GATHER_TILE
    assert nsub <= META_HI
    first_blk = [jnp.zeros((N_EXPERTS, 1), i32) for _ in range(nsub)]
    last_blk = [jnp.zeros((N_EXPERTS, 1), i32) for _ in range(nsub)]
    for j in range(t // blk):
        sl = slice(j * blk, (j + 1) * blk)
        m = jnp.concatenate([gtf[:, sl], eqf[:, sl]], axis=0)
        pc = _dot(m.astype(bf16), tri) + carry
        carry = pc[:, blk - 1:blk]
        cs_gt = pc[:N_EXPERTS].astype(i32)
        cs_eq = pc[N_EXPERTS:].astype(i32)
        sel = gt[:, sl] | (eq[:, sl] & (cs_eq <= need))
        cs = cs_gt + jnp.minimum(cs_eq, need)
        key_ref[:, sl] = jnp.where(sel, cs, 0)
        end = cs[:, blk - 1:blk]
        start = before & ~7
        start_ref[:, j:j + 1] = start
        rounds = (end - start + (window - 1)) // window
        meta_ref[:, META_NROUND + j:META_NROUND + j + 1] = jnp.broadcast_to(
            jnp.max(rounds, axis=0, keepdims=True), (N_EXPERTS, 1))
        for s in range(nsub):
            first_blk[s] = first_blk[s] + jnp.where(end <= s * GATHER_TILE, 1, 0)
            last_blk[s] = last_blk[s] + jnp.where(end < (s + 1) * GATHER_TILE, 1, 0)
        before = end
    for s in range(nsub):
        meta_ref[:, s:s + 1] = first_blk[s]
        meta_ref[:, META_HI + s:META_HI + s + 1] = last_blk[s]


def moe_route(aff_t, cap, window):
    t = aff_t.shape[1]
    small = jax.ShapeDtypeStruct((N_EXPERTS, 128), i32)
    small_spec = pl.BlockSpec((N_EXPERTS, 128), lambda i: (0, 0))
    return pl.pallas_call(
        functools.partial(_route_kernel, cap=cap, window=window), grid=(1,),
        in_specs=[pl.BlockSpec((N_EXPERTS, t), lambda i: (0, 0))],
        out_specs=[pl.BlockSpec((N_EXPERTS, t), lambda i: (0, 0)), small_spec, small_spec],
        out_shape=[jax.ShapeDtypeStruct((N_EXPERTS, t), i32), small, small],
        compiler_params=_cparams("arbitrary"), name="moe_route",
    )(aff_t)


def _moe_ffn_kernel(meta_ref, key_ref, aff_ref, h_ref, wg_ref, wu_ref, wd_ref, o_ref, acc_ref, gate_ref, *, gb):
    e = pl.program_id(0)
    j = pl.program_id(1)
    tm = acc_ref.shape[0]
    tb = TOKEN_BLOCK
    acc_ref[...] = jnp.zeros_like(acc_ref)
    gate_ref[...] = jnp.zeros_like(gate_ref)
    gt = GATHER_TILE
    nsub = tm // gt
    nblk = key_ref.shape[1]
    row_id = lax.broadcasted_iota(i32, (gt, 1), 0)
    for s in range(nsub):
        sub = j * nsub + s
        srows = slice(s * gt, (s + 1) * gt)
        first = meta_ref[e, sub]
        last = meta_ref[e, META_HI + sub]

        def window(w, carry, sub=sub, srows=srows, first=first):
            begin = first + w * gb
            start = jnp.minimum(begin, nblk - gb)
            pieces = []
            gate = jnp.zeros((gt, 1), f32)
            for k in range(gb):
                b = start + k
                slot = jnp.where(b >= begin, sub * gt + 1, -gt) + row_id
                oh = key_ref[0, b] == slot
                pieces.append(jnp.where(oh, 1.0, 0.0).astype(bf16))
                gate = gate + jnp.sum(jnp.where(oh, aff_ref[0, b], 0.0), axis=1, keepdims=True)
            rows = pl.ds(pl.multiple_of(start * tb, tb), gb * tb)
            acc_ref[srows, :] += _dot(jnp.concatenate(pieces, axis=1), h_ref[rows, :])
            gate_ref[srows, :] += gate
            return carry

        lax.fori_loop(0, (last - first) // gb + 1, window, 0)
    xe = acc_ref[...].astype(bf16)
    hid = _silu(_dot(xe, wg_ref[0, 0].astype(bf16))) * _dot(xe, wu_ref[0, 0].astype(bf16))
    o_ref[0] = _dot(hid.astype(bf16), wd_ref[0, 0].astype(bf16)) * gate_ref[...]


def moe_ffn(meta, key, aff_t, h, wg, wu, wd, layer, cap, gather_blocks):
    t = h.shape[0]
    tm = SLOT_TILE
    nblk = t // TOKEN_BLOCK
    nj = cap // tm
    key4 = key.reshape(N_EXPERTS, nblk, 1, TOKEN_BLOCK)
    aff4 = aff_t.reshape(N_EXPERTS, nblk, 1, TOKEN_BLOCK)
    row_spec = pl.BlockSpec((1, nblk, 1, TOKEN_BLOCK), lambda e, j, *_: (e, 0, 0, 0))
    wspec = pl.BlockSpec((1, 1, D, D_FF), lambda e, j, *_: (layer, e, 0, 0))
    grid_spec = pltpu.PrefetchScalarGridSpec(
        num_scalar_prefetch=1, grid=(N_EXPERTS, nj),
        in_specs=[row_spec, row_spec,
                  pl.BlockSpec((t, D), lambda e, j, *_: (0, 0), pipeline_mode=pl.Buffered(1)),
                  wspec, wspec, pl.BlockSpec((1, 1, D_FF, D), lambda e, j, *_: (layer, e, 0, 0))],
        out_specs=pl.BlockSpec((1, tm, D), lambda e, j, *_: (e, j, 0)),
        scratch_shapes=[pltpu.VMEM((tm, D), f32), pltpu.VMEM((tm, 1), f32)])
    return pl.pallas_call(
        functools.partial(_moe_ffn_kernel, gb=gather_blocks), grid_spec=grid_spec,
        out_shape=jax.ShapeDtypeStruct((N_EXPERTS, cap, D), f32),
        compiler_params=_cparams("arbitrary", "arbitrary"), name="moe_ffn",
    )(meta, key4, aff4, h, wg, wu, wd)


def _moe_combine_kernel(start_ref, meta_ref, key_ref, ye_hbm, x_ref, mod_ref, fg_ref, o_ref, stage_ref, acc_ref, sem,
                        *, cap, nblk, window, final_norm):
    i = pl.program_id(0)
    w = window

    def window_start(tile, e, r):
        return start_ref[e, tile] + r * w

    def copies(tile, r, buf):
        out = []
        for e in range(N_EXPERTS):
            first = pl.multiple_of(jnp.minimum(window_start(tile, e, r), cap - w), 8)
            out.append(pltpu.make_async_copy(ye_hbm.at[e, pl.ds(first, w), :],
                                             stage_ref.at[buf, pl.ds(e * w, w), :], sem.at[buf]))
        return out

    def accumulate(r, buf):
        pieces = []
        for e in range(N_EXPERTS):
            begin = window_start(i, e, r)
            slot = jnp.minimum(begin, cap - w) + lax.broadcasted_iota(i32, (w, 1), 0) + 1
            oh = (key_ref[e:e + 1, :] == slot) & (slot > begin)
            pieces.append(jnp.where(oh, 1.0, 0.0))
        oh = jnp.concatenate(pieces, axis=0).astype(bf16)
        rows = stage_ref[buf]
        hi = rows.astype(bf16)
        lo = (rows - hi.astype(f32)).astype(bf16)
        acc_ref[...] += _dot_tn(oh, hi) + _dot_tn(oh, lo)

    cur = lax.rem(i, 2)

    @pl.when(i == 0)
    def _():
        for cp in copies(0, 0, 0):
            cp.start()

    @pl.when(i + 1 < nblk)
    def _():
        for cp in copies(i + 1, 0, 1 - cur):
            cp.start()

    acc_ref[...] = jnp.zeros_like(acc_ref)
    for cp in copies(i, 0, cur):
        cp.wait()
    accumulate(0, cur)

    def extra_round(r, carry):
        cps = copies(i, r, 2)
        for cp in cps:
            cp.start()
        for cp in cps:
            cp.wait()
        accumulate(r, 2)
        return carry

    lax.fori_loop(1, meta_ref[0, META_NROUND + i], extra_round, 0)
    out = x_ref[...] + mod_ref[0][GATE2:GATE2 + 1, :] * acc_ref[...]
    if final_norm:
        out = _rms(out, fg_ref[...])
    o_ref[...] = out


def moe_combine(starts, meta, key, ye, x, mod, final_g, cap, rows_per_mod, window, final_norm):
    t = x.shape[0]
    tb = TOKEN_BLOCK
    nblk = t // tb
    grid_spec = pltpu.PrefetchScalarGridSpec(
        num_scalar_prefetch=2, grid=(nblk,),
        in_specs=[pl.BlockSpec((N_EXPERTS, tb), lambda i, *_: (0, i)),
                  pl.BlockSpec(memory_space=pl.ANY),
                  pl.BlockSpec((tb, D), lambda i, *_: (i, 0)),
                  pl.BlockSpec((1, N_MOD, D), lambda i, *_: ((i * tb) // rows_per_mod, 0, 0)),
                  pl.BlockSpec((1, D), lambda i, *_: (0, 0))],
        out_specs=pl.BlockSpec((tb, D), lambda i, *_: (i, 0)),
        scratch_shapes=[pltpu.VMEM((3, N_EXPERTS * window, D), f32), pltpu.VMEM((tb, D), f32),
                        pltpu.SemaphoreType.DMA((3,))])
    return pl.pallas_call(
        functools.partial(_moe_combine_kernel, cap=cap, nblk=nblk, window=window, final_norm=final_norm),
        grid_spec=grid_spec,
        out_shape=jax.ShapeDtypeStruct((t, D), f32),
        compiler_params=_cparams("arbitrary"), name="moe_combine",
    )(starts, meta, key, ye, x, mod, final_g)


def moe_layer(x, g2, mod, w_router_t, wg, wu, wd, final_g, layer, rows_per_mod, window, gather_blocks):
    t = x.shape[0]
    cap = 2 * t // N_EXPERTS
    h, aff_t = moe_router(x, g2, mod, w_router_t, rows_per_mod)
    key, starts, meta = moe_route(aff_t, cap, window)
    ye = moe_ffn(meta, key, aff_t, h, wg, wu, wd, layer, cap, gather_blocks)
    return moe_combine(starts, meta, key, ye, x, mod, final_g, cap, rows_per_mod, window, layer == DEPTH - 1)


def _rope_tables(seq):
    rows = seq // GRID_W
    row = jnp.repeat(jnp.arange(rows), GRID_W).astype(f32)
    col = jnp.tile(jnp.arange(GRID_W), rows).astype(f32)
    pairs = QK_ROPE // 4
    inv = ROPE_THETA ** (-jnp.arange(pairs, dtype=f32) / pairs)
    ang = jnp.concatenate([row[:, None] * inv, col[:, None] * inv], axis=-1)
    cos, sin = jnp.cos(ang), jnp.sin(ang)
    zero = jnp.zeros_like(cos)
    c = jnp.concatenate([cos, cos, zero, zero], axis=-1)
    s1 = jnp.concatenate([-sin, zero, zero, zero], axis=-1)
    s2 = jnp.concatenate([zero, sin, zero, zero], axis=-1)
    return c, s1, s2


def _mla_weights(w_dq, q_norm_g, w_uq, w_dkv, kv_norm_g, w_ukv, w_o):
    per_head = QK_NOPE + QK_ROPE
    uq = w_uq.reshape(Q_LORA, MLA_HEADS, per_head)
    uq_nope = uq[..., :QK_NOPE].reshape(Q_LORA, MLA_HEADS * QK_NOPE)
    uq_rope = jnp.pad(uq[..., QK_NOPE:], ((0, 0), (0, 0), (0, HEAD_PAD - QK_ROPE))).reshape(Q_LORA, MLA_HEADS * HEAD_PAD)
    wuq = jnp.concatenate([uq_nope, uq_rope], axis=1).astype(bf16)
    wdkv = jnp.pad(w_dkv, ((0, 0), (0, HEAD_PAD - QK_ROPE))).astype(bf16)
    ukv = w_ukv.reshape(KV_LORA, MLA_HEADS, QK_NOPE + V_HEAD)
    wukv = jnp.concatenate([ukv[..., :QK_NOPE].reshape(KV_LORA, -1), ukv[..., QK_NOPE:].reshape(KV_LORA, -1)], axis=1).astype(bf16)
    proj = (w_dq.astype(bf16), q_norm_g.reshape(1, Q_LORA), wuq, wdkv, kv_norm_g.reshape(1, KV_LORA), wukv)
    return proj, wukv, w_o.astype(bf16)


def _mla_layer(xp, xs, modp, mods, g1, cache_ckv_j, cache_krope_j, w, rope_tabs):
    proj_w, wukv, w_o = w
    bp, bs = xp.shape[0] // 256, xs.shape[0] // 2048
    nh = MLA_HEADS * HEAD_PAD
    qn, qr, kn, v, kr, ckv, krope = mla_project(xp, g1, modp, proj_w, None, xp.shape[0])
    r3 = lambda a, b: a.reshape(b, -1, a.shape[-1])
    op = mla_attention(r3(qn, bp), r3(qr, bp), [(r3(kn, bp), r3(kr, bp), r3(v, bp))], nb=2).reshape(-1, nh)
    xp = matmul_residual(op, w_o, xp, modp, GATE1, xp.shape[0])
    new_ckv = ckv.reshape(bp, -1, KV_LORA)
    new_krope = krope[:, :QK_ROPE].reshape(bp, -1, QK_ROPE)
    qn, qr, kn, v, kr, _, _ = mla_project(xs, g1, mods, proj_w, rope_tabs, 2048)
    ctx = matmul_bf16(cache_ckv_j.reshape(-1, KV_LORA).astype(bf16), wukv, tm=512)
    nk = MLA_HEADS * QK_NOPE
    kr_ctx = jnp.pad(cache_krope_j, ((0, 0), (0, 0), (0, HEAD_PAD - QK_ROPE))).astype(bf16)
    segments = [(ctx[:, :nk].reshape(bs, -1, nk), kr_ctx, ctx[:, nk:].reshape(bs, -1, nk)),
                (r3(kn, bs), r3(kr, bs), r3(v, bs))]
    os_ = mla_attention(r3(qn, bs), r3(qr, bs), segments).reshape(-1, nh)
    xs = matmul_residual(os_, w_o, xs, mods, GATE1, 2048)
    return xp, xs, new_ckv, new_krope


def _ssm_stream(x, mod, g1, w, rows_per_mod, seq, h0, want_state=False, prev_states=None):
    w_all, conv_w_all, bias_all, a_log, d_skip_row, norm_g, w_out = w
    b = x.shape[0] // seq
    z, xbc, dt = ssm_in_proj(x, g1, mod, w_all, conv_w_all, bias_all, rows_per_mod, seq)
    dt4 = dt[:, :2 * SSM_H].reshape(b, seq, 2, SSM_H)
    dt_dir = dt4.transpose(2, 0, 1, 3)
    dtT_dir = dt4.transpose(2, 0, 3, 1)
    xbc3 = xbc.reshape(b, seq, CONV_DIM)
    bT = xbc3[:, :, D_INNER:D_INNER + SSM_G * SSM_N].transpose(0, 2, 1)
    outs = ssd_scan(xbc3, z.reshape(b, seq, D_INNER), dt_dir, dtT_dir, bT, a_log, d_skip_row, norm_g, h0,
                    want_state, prev_states, nb=2 if seq <= 2 * CHUNK else 1)
    x = matmul_residual(outs[0].reshape(-1, D_INNER), w_out, x, mod, GATE1, rows_per_mod)
    return x, (outs[1] if want_state else None)


def kernel(x_prompt, x_sample, cache_ckv, cache_krope, state_ssm, c, c_ctx, w_mod, b_mod, norm1_g, norm2_g, final_norm_g, mla_w_dq, mla_q_norm_g, mla_w_uq, mla_w_dkv, mla_kv_norm_g, mla_w_ukv, mla_w_o, ssm_w_in, ssm_conv_w, ssm_conv_b, ssm_dt_bias, ssm_a_log, ssm_d_skip, ssm_norm_g, ssm_w_out, moe_w_router, moe_w_gate, moe_w_up, moe_w_down):
    bp, lp, _ = x_prompt.shape
    bs, ls, _ = x_sample.shape
    xp = x_prompt.reshape(bp * lp, D)
    xs = x_sample.reshape(bs * ls, D)

    cond8 = jnp.concatenate([c_ctx[None, :], c, jnp.zeros((8 - 1 - bs, D), f32)], axis=0)
    mod_all = modulation_all(cond8, w_mod, b_mod)
    rope_tabs = _rope_tables(ls)

    new_ckv, new_krope = [], []
    new_ssm = None
    for l in range(DEPTH):
        modp = mod_all[l, 0:1].reshape(1, N_MOD, D)
        mods = mod_all[l, 1:1 + bs].reshape(bs, N_MOD, D)
        g1 = norm1_g[l].reshape(1, D)
        j = l // 2
        if l % 2 == 0:
            w = _mla_weights(mla_w_dq[j], mla_q_norm_g[j], mla_w_uq[j], mla_w_dkv[j], mla_kv_norm_g[j], mla_w_ukv[j], mla_w_o[j])
            xp, xs, ckv, krope = _mla_layer(xp, xs, modp, mods, g1, cache_ckv[:, j], cache_krope[:, j], w, rope_tabs)
            new_ckv.append(ckv)
            new_krope.append(krope)
        else:
            dt_cols = INPROJ_BLOCK - 2 * SSM_H
            w_all = jnp.pad(ssm_w_in[j], ((0, 0), (0, dt_cols))).astype(bf16)
            conv_w_all = jnp.pad(ssm_conv_w[j].T, ((0, 0), (D_INNER, INPROJ_BLOCK)))
            bias_all = jnp.concatenate([jnp.zeros((D_INNER,), f32), ssm_conv_b[j], ssm_dt_bias[j].reshape(-1),
                                        jnp.zeros((dt_cols,), f32)]).reshape(1, -1)
            w = (w_all, conv_w_all, bias_all, ssm_a_log[j],
                 jnp.repeat(ssm_d_skip[j], SSM_P).reshape(1, D_INNER), ssm_norm_g[j].reshape(1, D_INNER),
                 ssm_w_out[j].astype(bf16))
            xp, new_ssm = _ssm_stream(xp, modp, g1, w, bp * lp, lp, None, want_state=True, prev_states=new_ssm)
            h0 = state_ssm[:, j].transpose(0, 1, 4, 2, 3).reshape(bs, 2, SSM_N, D_INNER)
            xs, _ = _ssm_stream(xs, mods, g1, w, ls, ls, h0)
        g2 = norm2_g[l].reshape(1, D)
        moe_w = (moe_w_router[l].T, moe_w_gate, moe_w_up, moe_w_down, final_norm_g.reshape(1, D), l)
        xp = moe_layer(xp, g2, modp, *moe_w, bp * lp, COMBINE_WINDOW_PROMPT, GATHER_BLOCKS_PROMPT)
        xs = moe_layer(xs, g2, mods, *moe_w, ls, COMBINE_WINDOW_SAMPLE, GATHER_BLOCKS_SAMPLE)

    y_prompt = xp.reshape(bp, lp, D)
    y_sample = xs.reshape(bs, ls, D)
    new_ssm = new_ssm.reshape(bp, DEPTH // 2, 2, SSM_H, SSM_P, SSM_N)
    return (y_prompt, y_sample, jnp.stack(new_ckv, axis=1), jnp.stack(new_krope, axis=1), new_ssm)
```
